```python
import math
import jax, jax.numpy as jnp
from jax import lax
import numpy as np

D_MODEL = 1024
BATCH = 4
SEQ = 4096
DEPTH = 2
DEC_BATCH = 32
DEC_SEQ = 4
PAST_LEN = 8192
PAGE_SIZE = 128

N_A_LAYERS = DEPTH // 2
N_B_LAYERS = DEPTH - N_A_LAYERS

SSM_INNER = 2 * D_MODEL
SSM_HEAD_DIM = 64
SSM_HEADS = SSM_INNER // SSM_HEAD_DIM
SSM_GROUPS = 4
SSM_HEADS_PER_GROUP = SSM_HEADS // SSM_GROUPS
SSM_STATE = 128
CONV_WIDTH = 4
CONV_DIM = SSM_INNER + 2 * SSM_GROUPS * SSM_STATE
SSM_CHUNK = 128
SSM_IN_DIM = SSM_INNER + CONV_DIM + SSM_HEADS

ATT_HEAD_DIM = 128
KV_HEADS = D_MODEL // ATT_HEAD_DIM
WINDOWS = (128, 512, 2048)
DILATIONS = (1, 4, 16)
N_DIL_GROUPS = len(WINDOWS)
Q_HEADS = N_DIL_GROUPS * KV_HEADS
Q_WIDTH = Q_HEADS * ATT_HEAD_DIM
ATT_WIDTH = KV_HEADS * ATT_HEAD_DIM
B_IN_DIM = Q_WIDTH + ATT_WIDTH
WINDOW_MAX = max(WINDOWS)
N_BUCKETS = 32
MAX_DISTANCE = WINDOW_MAX

NORM_EPS = 1e-5
NEG_INF = -1e30

kernel_name = "yoco_ssd_dilated_window_step"


def rmsnorm(x, g):
    xf = x.astype(jnp.float32)
    y = xf * lax.rsqrt(jnp.mean(xf * xf, axis=-1, keepdims=True) + NORM_EPS)
    return (y * g.astype(jnp.float32)).astype(x.dtype)


def t5_bucket(dist):
    max_exact = N_BUCKETS // 2
    n = jnp.maximum(dist, 0)
    nf = jnp.maximum(n, 1).astype(jnp.float32)
    large = max_exact + (jnp.log(nf / max_exact) / math.log(MAX_DISTANCE / max_exact)
                         * (N_BUCKETS - max_exact)).astype(jnp.int32)
    large = jnp.minimum(large, N_BUCKETS - 1)
    return jnp.where(n < max_exact, n, large)


def group_bias(rel_bias, g):
    nw = WINDOWS[g] // DILATIONS[g]
    dist = jnp.arange(nw + 1, dtype=jnp.int32) * DILATIONS[g]
    cols = rel_bias[:, g * KV_HEADS:(g + 1) * KV_HEADS].astype(jnp.float32)
    return cols[t5_bucket(dist)]


def ssd_scan(xdt, adt, bm, cm, init):
    b, l, nh, p = xdt.shape
    T = min(SSM_CHUNK, l)
    nc = -(-l // T)
    lp = nc * T

    def pad(t):
        return jnp.pad(t, ((0, 0), (0, lp - l)) + ((0, 0),) * (t.ndim - 2))

    G, J = SSM_GROUPS, SSM_HEADS_PER_GROUP
    X = pad(xdt).reshape(b, nc, T, G, J, p)
    A = pad(adt).reshape(b, nc, T, G, J).transpose(0, 3, 4, 1, 2)
    Bc = pad(bm).reshape(b, nc, T, G, SSM_STATE)
    Cc = pad(cm).reshape(b, nc, T, G, SSM_STATE)
    a_cs = jnp.cumsum(A, axis=-1)
    tri = jnp.tril(jnp.ones((T, T), dtype=bool))
    seg = jnp.exp(jnp.where(tri, a_cs[..., :, None] - a_cs[..., None, :], -jnp.inf))
    cb = jnp.einsum("bctgn,bcsgn->bcgts", Cc, Bc)
    m = jnp.einsum("bcgts,bgjcts->bcgjts", cb, seg)
    y_diag = jnp.einsum("bcgjts,bcsgjp->bctgjp", m, X)
    decay_in = jnp.exp(a_cs[..., -1:] - a_cs)
    chunk_states = jnp.einsum("bctgn,bgjct,bctgjp->cbgjpn", Bc, decay_in, X)
    chunk_decay = jnp.exp(a_cs[..., -1]).transpose(3, 0, 1, 2)
    s0 = init.reshape(b, G, J, p, SSM_STATE)

    def step(s, inp):
        dec, cs = inp
        return dec[..., None, None] * s + cs, s

    s_final, s_in = lax.scan(step, s0, (chunk_decay, chunk_states))
    y_off = jnp.einsum("bctgn,cbgjpn,bgjct->bctgjp", Cc, s_in, jnp.exp(a_cs))
    y = (y_diag + y_off).reshape(b, lp, nh, p)[:, :l]
    return y, s_final.reshape(b, nh, p, SSM_STATE)


def ssd_mixer(h, conv_prev, ssm_init, g_norm, w_in, conv_w, conv_b, dt_bias, a_log, d_skip,
              g_gate, w_out):
    b, l, _ = h.shape
    u = rmsnorm(h, g_norm) @ w_in
    z = u[..., :SSM_INNER]
    xbc = u[..., SSM_INNER:SSM_INNER + CONV_DIM]
    dt_raw = u[..., SSM_INNER + CONV_DIM:]
    xc = jnp.concatenate([conv_prev.astype(xbc.dtype), xbc], axis=1)
    conv = conv_b
    for k in range(CONV_WIDTH):
        conv = conv + xc[:, k:k + l] * conv_w[k]
    new_conv = xc[:, l:]
    act = jax.nn.silu(conv.astype(jnp.float32))
    xs = act[..., :SSM_INNER].reshape(b, l, SSM_HEADS, SSM_HEAD_DIM)
    gn = SSM_GROUPS * SSM_STATE
    bm = act[..., SSM_INNER:SSM_INNER + gn].reshape(b, l, SSM_GROUPS, SSM_STATE)
    cm = act[..., SSM_INNER + gn:].reshape(b, l, SSM_GROUPS, SSM_STATE)
    dt = jax.nn.softplus(dt_raw.astype(jnp.float32) + dt_bias.astype(jnp.float32))
    a = -jnp.exp(a_log.astype(jnp.float32))
    y, s_final = ssd_scan(xs * dt[..., None], dt * a, bm, cm, ssm_init.astype(jnp.float32))
    y = y + xs * d_skip.astype(jnp.float32)[:, None]
    y = rmsnorm(y.reshape(b, l, SSM_INNER) * jax.nn.silu(z.astype(jnp.float32)), g_gate)
    return y.astype(h.dtype) @ w_out, new_conv, s_final


def dilated_attn_prompt(q, k, v, bvec, dil, nw):
    b, s, h, dh = q.shape
    m = s // dil
    lb = nw
    nb = -(-m // lb)
    mp = nb * lb

    def to_sub(t):
        t = t.reshape(b, m, dil, h, dh).transpose(0, 2, 1, 3, 4)
        return jnp.pad(t, ((0, 0), (0, 0), (0, mp - m), (0, 0), (0, 0)))

    def key_blocks(t):
        t = jnp.pad(to_sub(t), ((0, 0), (0, 0), (lb, 0), (0, 0), (0, 0)))
        t = t.reshape(b, dil, nb + 1, lb, h, dh)
        return jnp.concatenate([t[:, :, :-1], t[:, :, 1:]], axis=3)

    qs = to_sub(q).reshape(b, dil, nb, lb, h, dh)
    kb = key_blocks(k)
    vb = key_blocks(v)
    scores = jnp.einsum("brnqhd,brnkhd->brnhqk", qs, kb) * (dh ** -0.5)
    rel = lb + jnp.arange(lb)[:, None] - jnp.arange(2 * lb)[None, :]
    band = (rel >= 0) & (rel <= nw)
    bias = bvec[jnp.clip(rel, 0, nw)].transpose(2, 0, 1)
    key_ok = (jnp.arange(nb)[:, None] - 1) * lb + jnp.arange(2 * lb)[None, :] >= 0
    valid = band[None, None] & key_ok[:, None, None, :]
    scores = jnp.where(valid, scores + bias, NEG_INF)
    lse = jax.nn.logsumexp(scores, axis=-1)
    p = jnp.exp(scores - lse[..., None])
    o = jnp.einsum("brnhqk,brnkhd->brnqhd", p, vb)
    o = o.reshape(b, dil, mp, h, dh)[:, :, :m].transpose(0, 2, 1, 3, 4).reshape(b, s, h, dh)
    lse = lse.transpose(0, 1, 2, 4, 3).reshape(b, dil, mp, h)[:, :, :m]
    lse = lse.transpose(0, 2, 1, 3).reshape(b, s, h)
    return o, lse


def dilated_attn_sample(q, kc, vc, bvec, dil, nw, n_old):
    t = q.shape[1]
    idx = n_old + jnp.arange(t)[:, None] - dil * jnp.arange(nw + 1)[None, :]
    valid = idx >= 0
    idx = jnp.maximum(idx, 0)
    kg = kc[:, idx]
    vg = vc[:, idx]
    scores = jnp.einsum("bthd,btjhd->bthj", q, kg) * (q.shape[-1] ** -0.5) + bvec.T[None, None]
    scores = jnp.where(valid[None, :, None, :], scores, NEG_INF)
    lse = jax.nn.logsumexp(scores, axis=-1)
    p = jnp.exp(scores - lse[..., None])
    o = jnp.einsum("bthj,btjhd->bthd", p, vg)
    return o, lse


def dilated_mixer(h, attend, biases, g_norm, w_in, w_out):
    b, l, _ = h.shape
    u = rmsnorm(h, g_norm) @ w_in
    q = u[..., :Q_WIDTH].reshape(b, l, N_DIL_GROUPS, KV_HEADS, ATT_HEAD_DIM).astype(jnp.float32)
    gate = u[..., Q_WIDTH:].astype(jnp.float32)
    outs, lses = [], []
    for g in range(N_DIL_GROUPS):
        o, lse = attend(q[:, :, g], biases[g], DILATIONS[g], WINDOWS[g] // DILATIONS[g])
        outs.append(o)
        lses.append(lse)
    w = jax.nn.softmax(jnp.stack(lses, axis=0), axis=0)
    o = jnp.sum(w[..., None] * jnp.stack(outs, axis=0), axis=0)
    y = o.reshape(b, l, ATT_WIDTH) * jax.nn.silu(gate)
    return y.astype(h.dtype) @ w_out


def shared_kv(h, g, w_kv):
    b, l, _ = h.shape
    return (rmsnorm(h, g) @ w_kv).reshape(b, l, 2, KV_HEADS, ATT_HEAD_DIM)


def setup_inputs(seed: int = 0) -> dict:
    key = jax.random.key(seed)
    ks = jax.random.split(key, 24)
    nrm = jax.random.normal
    f32 = jnp.float32
    n_old = min(WINDOW_MAX, PAST_LEN)
    dt0 = jnp.exp(jax.random.uniform(ks[0], (N_A_LAYERS, SSM_HEADS), f32)
                  * (math.log(0.1) - math.log(0.001)) + math.log(0.001))
    return {
        "x_prompt": nrm(ks[1], (BATCH, SEQ, D_MODEL), f32),
        "x_sample": nrm(ks[2], (DEC_BATCH, DEC_SEQ, D_MODEL), f32),
        "state_ssm": 0.5 * nrm(ks[3], (N_A_LAYERS, DEC_BATCH, SSM_HEADS, SSM_HEAD_DIM, SSM_STATE), f32),
        "state_conv": nrm(ks[4], (N_A_LAYERS, DEC_BATCH, CONV_WIDTH - 1, CONV_DIM), f32),
        "cache_kv": nrm(ks[5], (DEC_BATCH, n_old, 2, KV_HEADS, ATT_HEAD_DIM), f32),
        "a_norm": 1.0 + 0.02 * nrm(ks[6], (N_A_LAYERS, D_MODEL), f32),
        "a_w_in": nrm(ks[7], (N_A_LAYERS, D_MODEL, SSM_IN_DIM), f32) * D_MODEL ** -0.5,
        "a_conv_w": nrm(ks[8], (N_A_LAYERS, CONV_WIDTH, CONV_DIM), f32) * CONV_WIDTH ** -0.5,
        "a_conv_b": 0.02 * nrm(ks[9], (N_A_LAYERS, CONV_DIM), f32),
        "a_dt_bias": dt0 + jnp.log(-jnp.expm1(-dt0)),
        "a_A_log": jnp.log(jax.random.uniform(ks[10], (N_A_LAYERS, SSM_HEADS), f32, 1.0, 16.0)),
        "a_D": 1.0 + 0.02 * nrm(ks[11], (N_A_LAYERS, SSM_HEADS), f32),
        "a_gate_norm": 1.0 + 0.02 * nrm(ks[12], (N_A_LAYERS, SSM_INNER), f32),
        "a_w_out": nrm(ks[13], (N_A_LAYERS, SSM_INNER, D_MODEL), f32) * SSM_INNER ** -0.5,
        "rel_bias": 0.5 * nrm(ks[14], (N_BUCKETS, Q_HEADS), f32),
        "kv_norm": 1.0 + 0.02 * nrm(ks[15], (D_MODEL,), f32),
        "w_kv": nrm(ks[16], (D_MODEL, 2 * ATT_WIDTH), f32) * D_MODEL ** -0.5,
        "b_norm": 1.0 + 0.02 * nrm(ks[17], (N_B_LAYERS, D_MODEL), f32),
        "b_w_in": nrm(ks[18], (N_B_LAYERS, D_MODEL, B_IN_DIM), f32) * D_MODEL ** -0.5,
        "b_w_out": nrm(ks[19], (N_B_LAYERS, ATT_WIDTH, D_MODEL), f32) * ATT_WIDTH ** -0.5,
        "final_norm": 1.0 + 0.02 * nrm(ks[20], (D_MODEL,), f32),
    }


def reference(x_prompt, x_sample, state_ssm, state_conv, cache_kv, a_norm, a_w_in, a_conv_w,
              a_conv_b, a_dt_bias, a_A_log, a_D, a_gate_norm, a_w_out, rel_bias, kv_norm, w_kv,
              b_norm, b_w_in, b_w_out, final_norm):
    hp, hs = x_prompt, x_sample
    bp, sp = x_prompt.shape[0], x_prompt.shape[1]
    n_old = cache_kv.shape[1]
    biases = [group_bias(rel_bias, g) for g in range(N_DIL_GROUPS)]
    ssm_p, ssm_s, conv_p, conv_s = [], [], [], []
    attend_p = attend_s = None
    kv_p = kv_s = None
    for layer in range(DEPTH):
        if layer < N_A_LAYERS:
            i = layer
            params = (a_norm[i], a_w_in[i], a_conv_w[i], a_conv_b[i], a_dt_bias[i], a_A_log[i],
                      a_D[i], a_gate_norm[i], a_w_out[i])
            conv0 = jnp.zeros((bp, CONV_WIDTH - 1, CONV_DIM), hp.dtype)
            ssm0 = jnp.zeros((bp, SSM_HEADS, SSM_HEAD_DIM, SSM_STATE), jnp.float32)
            o, c, s = ssd_mixer(hp, conv0, ssm0, *params)
            hp = hp + o
            conv_p.append(c)
            ssm_p.append(s)
            o, c, s = ssd_mixer(hs, state_conv[i], state_ssm[i], *params)
            hs = hs + o
            conv_s.append(c)
            ssm_s.append(s)
            if layer == N_A_LAYERS - 1:
                kv_p = shared_kv(hp, kv_norm, w_kv)
                kv_s = shared_kv(hs, kv_norm, w_kv)
                k_p = kv_p[:, :, 0].astype(jnp.float32)
                v_p = kv_p[:, :, 1].astype(jnp.float32)
                kc = jnp.concatenate([cache_kv[:, :, 0].astype(jnp.float32),
                                      kv_s[:, :, 0].astype(jnp.float32)], axis=1)
                vc = jnp.concatenate([cache_kv[:, :, 1].astype(jnp.float32),
                                      kv_s[:, :, 1].astype(jnp.float32)], axis=1)
                attend_p = (lambda q, bv, d, nw, k_p=k_p, v_p=v_p:
                            dilated_attn_prompt(q, k_p, v_p, bv, d, nw))
                attend_s = (lambda q, bv, d, nw, kc=kc, vc=vc:
                            dilated_attn_sample(q, kc, vc, bv, d, nw, n_old))
        else:
            j = layer - N_A_LAYERS
            hp = hp + dilated_mixer(hp, attend_p, biases, b_norm[j], b_w_in[j], b_w_out[j])
            hs = hs + dilated_mixer(hs, attend_s, biases, b_norm[j], b_w_in[j], b_w_out[j])
    y_prompt = rmsnorm(hp, final_norm)
    y_sample = rmsnorm(hs, final_norm)
    ssm_prompt = jnp.stack(ssm_p, axis=0).astype(state_ssm.dtype)
    ssm_sample = jnp.stack(ssm_s, axis=0).astype(state_ssm.dtype)
    conv_prompt = jnp.stack(conv_p, axis=0).astype(state_conv.dtype)
    conv_sample = jnp.stack(conv_s, axis=0).astype(state_conv.dtype)
    kv_prompt = kv_p[:, sp - min(WINDOW_MAX, sp):].astype(cache_kv.dtype)
    kv_sample = kv_s.astype(cache_kv.dtype)
    return (y_prompt, y_sample, ssm_prompt, ssm_sample, conv_prompt, conv_sample, kv_prompt, kv_sample)
```

```python
import functools
import math

import jax
import jax.numpy as jnp
from jax import lax
from jax.experimental import pallas as pl
from jax.experimental.pallas import tpu as pltpu

F32 = jnp.float32
BF16 = jnp.bfloat16

D_MODEL = 1024
SSM_INNER = 2048
SSM_HEADS = 32
SSM_HEAD_DIM = 64
SSM_GROUPS = 4
SSM_HEADS_PER_GROUP = SSM_HEADS // SSM_GROUPS
SSM_STATE = 128
CONV_WIDTH = 4
CONV_DIM = SSM_INNER + 2 * SSM_GROUPS * SSM_STATE
SSM_CHUNK = 128
ATT_HEAD_DIM = 128
KV_HEADS = 8
WINDOWS = (128, 512, 2048)
DILATIONS = (1, 4, 16)
N_DIL_GROUPS = 3
N_WIN = 128
Q_WIDTH = N_DIL_GROUPS * KV_HEADS * ATT_HEAD_DIM
ATT_WIDTH = KV_HEADS * ATT_HEAD_DIM
N_BUCKETS = 32
MAX_DISTANCE = max(WINDOWS)
NORM_EPS = 1e-5
NEG_INF = -1e30

LANES = 128
SUBLANES = 8
VMEM_LIMIT_BYTES = 56 * 1024 * 1024

assert all(w // d == N_WIN for w, d in zip(WINDOWS, DILATIONS))


def _round_up(x, m):
    return (x + m - 1) // m * m


def _dot(a, b):
    return jnp.dot(a, b, preferred_element_type=F32)


def _dot_nt(a, b):
    return lax.dot_general(a, b, (((1,), (1,)), ((), ())), preferred_element_type=F32)


def _dot_tn(a, b):
    return lax.dot_general(a, b, (((0,), (0,)), ((), ())), preferred_element_type=F32)


def _rms_scale(x):
    return lax.rsqrt(jnp.mean(x * x, axis=-1, keepdims=True) + NORM_EPS)


def _silu(x):
    return x * (1.0 / (1.0 + jnp.exp(-x)))


def _softplus(x):
    return jnp.maximum(x, 0.0) + jnp.log1p(jnp.exp(-jnp.abs(x)))


def _split3(v):
    h1 = v.astype(BF16)
    r1 = v - h1.astype(F32)
    h2 = r1.astype(BF16)
    h3 = (r1 - h2.astype(F32)).astype(BF16)
    return h1, h2, h3


def _const_spec(shape):
    nd = len(shape)
    return pl.BlockSpec(shape, lambda *_: (0,) * nd, pipeline_mode=pl.Buffered(1))


def _params(*sem):
    return pltpu.CompilerParams(dimension_semantics=sem, vmem_limit_bytes=VMEM_LIMIT_BYTES)


def _a_in_kernel(x_ref, g_ref, w_ref, cw_ref, cb_ref, dtb_ref, a_ref, cprev_ref,
                 z_ref, act_ref, dt_ref, da_ref, cstate_ref, xc_ref, *, shift):
    tm = x_ref.shape[1]
    hist = (CONV_WIDTH - 1) * shift
    pad = _round_up(hist, SUBLANES)

    x = x_ref[0]
    xn = (x * _rms_scale(x) * g_ref[...]).astype(BF16)
    z_ref[0] = _dot(xn, w_ref[:, :SSM_INNER])

    @pl.when(pl.program_id(1) == 0)
    def _():
        xc_ref[pad - hist:pad, :] = cprev_ref[0]

    xc_ref[pad:pad + tm, :] = _dot(xn, w_ref[:, SSM_INNER:SSM_INNER + CONV_DIM])
    conv = cb_ref[...]
    for k in range(CONV_WIDTH):
        conv = conv + xc_ref[pl.ds(pad - hist + k * shift, tm), :] * cw_ref[k:k + 1, :]
    act_ref[0] = _silu(conv)

    dt = _softplus(_dot(xn, w_ref[:, SSM_INNER + CONV_DIM:]) + dtb_ref[...])
    dt_ref[0] = dt
    da_ref[0] = jnp.exp(dt * a_ref[...])

    tail = xc_ref[pad + tm - hist:pad + tm, :]
    cstate_ref[0] = tail
    xc_ref[pad - hist:pad, :] = tail


def _a_in_proj(x, conv_prev, g, w, cw, cb, dtb, a_pad, *, tm, shift):
    nb, rows, _ = x.shape
    hist = (CONV_WIDTH - 1) * shift
    pad = _round_up(hist, SUBLANES)
    row_blk = lambda width: pl.BlockSpec((1, tm, width), lambda b, i: (b, i, 0))
    return pl.pallas_call(
        functools.partial(_a_in_kernel, shift=shift),
        grid=(nb, rows // tm),
        in_specs=[
            row_blk(D_MODEL),
            _const_spec((1, D_MODEL)),
            _const_spec(w.shape),
            _const_spec((CONV_WIDTH, CONV_DIM)),
            _const_spec((1, CONV_DIM)),
            _const_spec((1, LANES)),
            _const_spec((1, LANES)),
            pl.BlockSpec((1, hist, CONV_DIM), lambda b, i: (b, 0, 0)),
        ],
        out_specs=[
            row_blk(SSM_INNER), row_blk(CONV_DIM), row_blk(LANES), row_blk(LANES),
            pl.BlockSpec((1, hist, CONV_DIM), lambda b, i: (b, 0, 0)),
        ],
        out_shape=[
            jax.ShapeDtypeStruct((nb, rows, SSM_INNER), F32),
            jax.ShapeDtypeStruct((nb, rows, CONV_DIM), F32),
            jax.ShapeDtypeStruct((nb, rows, LANES), F32),
            jax.ShapeDtypeStruct((nb, rows, LANES), F32),
            jax.ShapeDtypeStruct((nb, hist, CONV_DIM), F32),
        ],
        scratch_shapes=[pltpu.VMEM((pad + tm, CONV_DIM), F32)],
        compiler_params=_params("arbitrary", "arbitrary"),
        name="a_in_proj",
    )(x, g, w, cw, cb, dtb, a_pad, conv_prev)


def _ssd_chunk_kernel(act_ref, dt_ref, a_ref, dexp_ref, e3_ref, y_ref, st_ref, sT_ref, ex_ref):
    T = SSM_CHUNK
    c = pl.program_id(1)

    @pl.when(c == 0)
    def _():
        sT_ref[...] = jnp.zeros_like(sT_ref)

    dt = dt_ref[0]
    adt = dt * a_ref[...]
    row = lax.broadcasted_iota(jnp.int32, (T, T), 0)
    col = lax.broadcasted_iota(jnp.int32, (T, T), 1)
    tri = row >= col
    ones_tri = jnp.where(tri, 1.0, 0.0).astype(BF16)
    p1, p2, p3 = _split3(adt)
    a_cs = _dot(ones_tri, p1) + _dot(ones_tri, p2) + _dot(ones_tri, p3)
    a_last = a_cs[T - 1:T, :]
    ea = jnp.exp(a_cs)
    wdt = jnp.exp(a_last - a_cs) * dt
    a_cs_t = a_cs.T
    dt_t = dt.T
    q1, q2, q3 = _split3(jnp.concatenate([ea, wdt], axis=0))
    ex_ref[...] = _dot(jnp.concatenate([q1, q2, q3], axis=1), e3_ref[...])

    lane = lax.broadcasted_iota(jnp.int32, (T, LANES), 1)
    gw = SSM_HEADS_PER_GROUP * SSM_HEAD_DIM
    for g in range(SSM_GROUPS):
        gs = slice(g * gw, (g + 1) * gw)
        b_off = SSM_INNER + g * SSM_STATE
        c_off = SSM_INNER + SSM_GROUPS * SSM_STATE + g * SSM_STATE
        bb = act_ref[0, :, b_off:b_off + SSM_STATE].astype(BF16)
        cc = act_ref[0, :, c_off:c_off + SSM_STATE].astype(BF16)
        cb = _dot_nt(cc, bb)
        s_old = sT_ref[:, gs]
        y_off = _dot(cc, s_old.astype(BF16)) * ex_ref[0:T, gs]
        xw = (act_ref[0, :, gs] * ex_ref[T:2 * T, gs]).astype(BF16)
        sT_ref[:, gs] = s_old * ex_ref[T - 1:T, gs] + _dot_tn(bb, xw)
        for jp in range(SSM_HEADS_PER_GROUP // 2):
            h0 = g * SSM_HEADS_PER_GROUP + 2 * jp
            ms = []
            for h in (h0, h0 + 1):
                seg = jnp.where(tri, jnp.exp(a_cs[:, h:h + 1] - a_cs_t[h:h + 1, :]), 0.0)
                ms.append((cb * seg * dt_t[h:h + 1, :]).astype(BF16))
            ls = slice(h0 * SSM_HEAD_DIM, h0 * SSM_HEAD_DIM + LANES)
            xp = act_ref[0, :, ls]
            rhs = jnp.concatenate([jnp.where(lane < SSM_HEAD_DIM, xp, 0.0).astype(BF16),
                                   jnp.where(lane >= SSM_HEAD_DIM, xp, 0.0).astype(BF16)], axis=0)
            y_diag = _dot(jnp.concatenate(ms, axis=1), rhs)
            y_ref[0, :, ls] = (y_diag + y_off[:, 2 * jp * SSM_HEAD_DIM:2 * jp * SSM_HEAD_DIM + LANES]
                               + xp * dexp_ref[:, ls])

    @pl.when(c == pl.num_programs(1) - 1)
    def _():
        for k in range(SSM_INNER // LANES):
            st_ref[0, k * LANES:(k + 1) * LANES, :] = sT_ref[:, k * LANES:(k + 1) * LANES].T


def _ssd_prompt(act, dt, a_pad, d_exp, e3):
    nb, seq, _ = act.shape
    T = SSM_CHUNK
    return pl.pallas_call(
        _ssd_chunk_kernel,
        grid=(nb, seq // T),
        in_specs=[
            pl.BlockSpec((1, T, CONV_DIM), lambda b, c: (b, c, 0)),
            pl.BlockSpec((1, T, LANES), lambda b, c: (b, c, 0)),
            _const_spec((1, LANES)),
            _const_spec((1, SSM_INNER)),
            _const_spec(e3.shape),
        ],
        out_specs=[
            pl.BlockSpec((1, T, SSM_INNER), lambda b, c: (b, c, 0)),
            pl.BlockSpec((1, SSM_INNER, SSM_STATE), lambda b, c: (b, 0, 0)),
        ],
        out_shape=[
            jax.ShapeDtypeStruct((nb, seq, SSM_INNER), F32),
            jax.ShapeDtypeStruct((nb, SSM_INNER, SSM_STATE), F32),
        ],
        scratch_shapes=[pltpu.VMEM((SSM_STATE, SSM_INNER), F32),
                        pltpu.VMEM((2 * T, SSM_INNER), F32)],
        compiler_params=_params("arbitrary", "arbitrary"),
        name="ssd_prompt",
    )(act, dt, a_pad, d_exp, e3)


def _ssd_step_kernel(dt_ref, da_ref, st_ref, xm_ref, bc_ref, dl_ref, y_ref, ns_ref, *, steps):
    b = pl.program_id(0)
    lane = lax.broadcasted_iota(jnp.int32, (SSM_HEAD_DIM, LANES), 1)
    xm = xm_ref[0]
    y = xm * dl_ref[...]
    gn = SSM_GROUPS * SSM_STATE
    for h in range(SSM_HEADS):
        g = h // SSM_HEADS_PER_GROUP
        s = st_ref[0, h]
        for t in range(steps):
            cidx = h * steps + t
            xcol = xm[:, cidx:cidx + 1] * dt_ref[b, cidx]
            brow = bc_ref[0, t:t + 1, g * SSM_STATE:(g + 1) * SSM_STATE]
            crow = bc_ref[0, t:t + 1, gn + g * SSM_STATE:gn + (g + 1) * SSM_STATE]
            s = s * da_ref[b, cidx] + xcol * brow
            ycol = jnp.sum(s * crow, axis=-1, keepdims=True)
            y = jnp.where(lane == cidx, y + ycol, y)
        ns_ref[0, h] = s
    y_ref[0] = y


def _ssd_sample(dt_s, da_s, state, xm, bc, d_lane, *, steps):
    nb = state.shape[0]
    smem = pl.BlockSpec(memory_space=pltpu.SMEM)
    st_spec = pl.BlockSpec((1, SSM_HEADS, SSM_HEAD_DIM, SSM_STATE), lambda b: (b, 0, 0, 0))
    xm_spec = pl.BlockSpec((1, SSM_HEAD_DIM, LANES), lambda b: (b, 0, 0))
    return pl.pallas_call(
        functools.partial(_ssd_step_kernel, steps=steps),
        grid=(nb,),
        in_specs=[smem, smem, st_spec, xm_spec,
                  pl.BlockSpec((1, steps, 2 * SSM_GROUPS * SSM_STATE), lambda b: (b, 0, 0)),
                  pl.BlockSpec((1, LANES), lambda b: (0, 0))],
        out_specs=[xm_spec, st_spec],
        out_shape=[jax.ShapeDtypeStruct(xm.shape, F32), jax.ShapeDtypeStruct(state.shape, F32)],
        compiler_params=_params("arbitrary"),
        name="ssd_sample",
    )(dt_s, da_s, state, xm, bc, d_lane)


def _mid_kernel(y_ref, z_ref, h_ref, gg_ref, wo_ref, kvn_ref, wkv_ref, bn_ref, wbi_ref,
                h1_ref, kv_ref, qg_ref):
    yg = y_ref[...] * _silu(z_ref[...])
    yn = (yg * _rms_scale(yg) * gg_ref[...]).astype(BF16)
    h1 = h_ref[...] + _dot(yn, wo_ref[...])
    h1_ref[...] = h1
    hn = h1 * _rms_scale(h1)
    kv_ref[...] = _dot((hn * kvn_ref[...]).astype(BF16), wkv_ref[...])
    qg_ref[...] = _dot((hn * bn_ref[...]).astype(BF16), wbi_ref[...])


def _mid_proj(y, z, h, gg, wo, kvn, wkv, bn, wbi, *, tm):
    rows = y.shape[0]
    row_blk = lambda width: pl.BlockSpec((tm, width), lambda i: (i, 0))
    return pl.pallas_call(
        _mid_kernel,
        grid=(rows // tm,),
        in_specs=[row_blk(SSM_INNER), row_blk(SSM_INNER), row_blk(D_MODEL),
                  _const_spec((1, SSM_INNER)), _const_spec(wo.shape),
                  _const_spec((1, D_MODEL)), _const_spec(wkv.shape),
                  _const_spec((1, D_MODEL)), _const_spec(wbi.shape)],
        out_specs=[row_blk(D_MODEL), row_blk(2 * ATT_WIDTH), row_blk(Q_WIDTH + ATT_WIDTH)],
        out_shape=[jax.ShapeDtypeStruct((rows, D_MODEL), F32),
                   jax.ShapeDtypeStruct((rows, 2 * ATT_WIDTH), F32),
                   jax.ShapeDtypeStruct((rows, Q_WIDTH + ATT_WIDTH), F32)],
        compiler_params=_params("arbitrary"),
        name="mid_proj",
    )(y, z, h, gg, wo, kvn, wkv, bn, wbi)


def _attn_prompt_kernel(q0_ref, q1_ref, q2_ref, k_ref, v_ref, bias_ref, o_ref, og_ref, lse_ref):
    seq = k_ref.shape[1]
    nw = N_WIN
    scale = ATT_HEAD_DIM ** -0.5
    q_refs = (q0_ref, q1_ref, q2_ref)

    def block(g, q_start, k_start, n_keys, bias):
        d = DILATIONS[g]
        qb = (q_refs[g][0, pl.ds(q_start, nw, stride=d), :] * scale).astype(BF16)
        kb = k_ref[0, pl.ds(k_start, n_keys, stride=d), :].astype(BF16)
        vb = v_ref[0, pl.ds(k_start, n_keys, stride=d), :].astype(BF16)
        s = _dot_nt(qb, kb) + bias
        m = jnp.max(s, axis=-1, keepdims=True)
        p = jnp.exp(s - m)
        l = jnp.sum(p, axis=-1, keepdims=True)
        o = _dot(p.astype(BF16), vb) * (1.0 / l)
        og_ref[g, pl.ds(q_start, nw, stride=d), :] = o
        lse_ref[g, pl.ds(q_start, nw, stride=d), :] = jnp.broadcast_to(m + jnp.log(l), (nw, LANES))

    for g in range(N_DIL_GROUPS):
        d = DILATIONS[g]
        n_blocks = seq // (d * nw)

        def per_class(r, carry, g=g, d=d, n_blocks=n_blocks):
            block(g, r, r, nw, bias_ref[g, 0, :, nw:])

            def per_block(n, carry2):
                block(g, r + d * nw * n, r + d * nw * (n - 1), 2 * nw, bias_ref[g, 0])
                return carry2

            return lax.fori_loop(1, n_blocks, per_block, carry)

        lax.fori_loop(0, d, per_class, 0)

    tile = 2 * nw
    def combine(i, carry):
        rs = pl.ds(pl.multiple_of(i * tile, tile), tile)
        l0, l1, l2 = lse_ref[0, rs, :], lse_ref[1, rs, :], lse_ref[2, rs, :]
        mx = jnp.maximum(jnp.maximum(l0, l1), l2)
        e0, e1, e2 = jnp.exp(l0 - mx), jnp.exp(l1 - mx), jnp.exp(l2 - mx)
        inv = 1.0 / (e0 + e1 + e2)
        o_ref[0, rs, :] = (e0 * og_ref[0, rs, :] + e1 * og_ref[1, rs, :] + e2 * og_ref[2, rs, :]) * inv
        return carry

    lax.fori_loop(0, seq // tile, combine, 0)


def _attn_prompt(qg, kv, bias):
    nb, seq, _ = kv.shape
    col_blk = lambda off: pl.BlockSpec((1, seq, ATT_HEAD_DIM), lambda b, h: (b, 0, off + h))
    return pl.pallas_call(
        _attn_prompt_kernel,
        grid=(nb, KV_HEADS),
        in_specs=[col_blk(0), col_blk(KV_HEADS), col_blk(2 * KV_HEADS),
                  col_blk(0), col_blk(KV_HEADS),
                  pl.BlockSpec((N_DIL_GROUPS, 1, N_WIN, 2 * N_WIN), lambda b, h: (0, h, 0, 0))],
        out_specs=col_blk(0),
        out_shape=jax.ShapeDtypeStruct((nb, seq, ATT_WIDTH), F32),
        scratch_shapes=[pltpu.VMEM((N_DIL_GROUPS, seq, ATT_HEAD_DIM), F32),
                        pltpu.VMEM((N_DIL_GROUPS, seq, LANES), F32)],
        compiler_params=_params("arbitrary", "arbitrary"),
        name="attn_prompt",
    )(qg, qg, qg, kv, kv, bias)


def _attn_sample_kernel(q_ref, new_ref, c0_ref, c1_ref, c2_ref, brev_ref, o_ref, w0_ref, *, steps):
    nw = N_WIN
    scale = ATT_HEAD_DIM ** -0.5
    w0_ref[0:nw] = c0_ref[...]
    w0_ref[nw:nw + steps] = new_ref[0]
    for t in range(steps):
        kx, vx = new_ref[0, t, 0], new_ref[0, t, 1]
        parts = []
        for g in range(N_DIL_GROUPS):
            if g == 0:
                ks, vs = w0_ref[t:t + nw, 0], w0_ref[t:t + nw, 1]
            else:
                c_ref = c1_ref if g == 1 else c2_ref
                ks, vs = c_ref[:, t, 0], c_ref[:, t, 1]
            q = q_ref[0, g * steps + t] * scale
            s = jnp.sum(ks * q[None], axis=-1, keepdims=True) + brev_ref[g, 0:nw]
            sx = jnp.sum(kx * q, axis=-1, keepdims=True) + brev_ref[g, nw]
            m = jnp.maximum(jnp.max(s, axis=0), sx)
            p = jnp.exp(s - m[None])
            px = jnp.exp(sx - m)
            parts.append((m, jnp.sum(p, axis=0) + px, jnp.sum(p * vs, axis=0) + px * vx))
        mx = jnp.maximum(jnp.maximum(parts[0][0], parts[1][0]), parts[2][0])
        den = jnp.zeros_like(mx)
        num = jnp.zeros_like(mx)
        for m, l, acc in parts:
            e = jnp.exp(m - mx)
            den = den + e * l
            num = num + e * acc
        o_ref[0, t] = num * (1.0 / den)


def _attn_sample(q, kv_new, cache, brev, *, steps):
    nb, n_old = cache.shape[0], cache.shape[1]
    nw = N_WIN
    assert n_old == nw * DILATIONS[2] and steps <= DILATIONS[1]
    tail = (2, KV_HEADS, ATT_HEAD_DIM)
    c1 = cache.reshape(nb, n_old // DILATIONS[1], DILATIONS[1], *tail)
    c2 = cache.reshape(nb, n_old // DILATIONS[2], DILATIONS[2], *tail)
    z3 = (0, 0, 0)
    return pl.pallas_call(
        functools.partial(_attn_sample_kernel, steps=steps),
        grid=(nb,),
        in_specs=[
            pl.BlockSpec((1, N_DIL_GROUPS * steps, KV_HEADS, ATT_HEAD_DIM), lambda b: (b, 0, 0, 0)),
            pl.BlockSpec((1, steps) + tail, lambda b: (b, 0) + z3),
            pl.BlockSpec((None, nw) + tail, lambda b: (b, n_old // nw - 1) + z3),
            pl.BlockSpec((None, nw, steps) + tail,
                         lambda b: (b, n_old // (nw * DILATIONS[1]) - 1, 0) + z3),
            pl.BlockSpec((None, nw, steps) + tail, lambda b: (b, 0, 0) + z3),
            pl.BlockSpec(brev.shape, lambda b: (0, 0, 0, 0)),
        ],
        out_specs=pl.BlockSpec((1, steps, KV_HEADS, ATT_HEAD_DIM), lambda b: (b, 0, 0, 0)),
        out_shape=jax.ShapeDtypeStruct((nb, steps, KV_HEADS, ATT_HEAD_DIM), F32),
        scratch_shapes=[pltpu.VMEM((nw + SUBLANES,) + tail, F32)],
        compiler_params=_params("arbitrary"),
        name="attn_sample",
    )(q, kv_new, cache, c1, c2, brev)


def _out_kernel(o_ref, gate_ref, h_ref, wo_ref, fn_ref, y_ref):
    yb = (o_ref[...] * _silu(gate_ref[...])).astype(BF16)
    h2 = h_ref[...] + _dot(yb, wo_ref[...])
    y_ref[...] = h2 * _rms_scale(h2) * fn_ref[...]


def _out_proj(o, qg, h1, wo, fn, *, tm):
    rows = o.shape[0]
    row_blk = pl.BlockSpec((tm, D_MODEL), lambda i: (i, 0))
    return pl.pallas_call(
        _out_kernel,
        grid=(rows // tm,),
        in_specs=[row_blk, pl.BlockSpec((tm, ATT_WIDTH), lambda i: (i, Q_WIDTH // ATT_WIDTH)),
                  row_blk, _const_spec(wo.shape), _const_spec((1, D_MODEL))],
        out_specs=row_blk,
        out_shape=jax.ShapeDtypeStruct((rows, D_MODEL), F32),
        compiler_params=_params("arbitrary"),
        name="out_proj",
    )(o, qg, h1, wo, fn)


def _t5_bucket(dist):
    max_exact = N_BUCKETS // 2
    n = jnp.maximum(dist, 0)
    nf = jnp.maximum(n, 1).astype(F32)
    large = max_exact + (jnp.log(nf / max_exact) / math.log(MAX_DISTANCE / max_exact)
                         * (N_BUCKETS - max_exact)).astype(jnp.int32)
    large = jnp.minimum(large, N_BUCKETS - 1)
    return jnp.where(n < max_exact, n, large)


def _group_bias(rel_bias, g):
    dist = jnp.arange(N_WIN + 1, dtype=jnp.int32) * DILATIONS[g]
    cols = rel_bias[:, g * KV_HEADS:(g + 1) * KV_HEADS].astype(F32)
    return cols[_t5_bucket(dist)]


def _bias_tables(rel_bias):
    nw = N_WIN
    rel = nw + jnp.arange(nw)[:, None] - jnp.arange(2 * nw)[None, :]
    band = (rel >= 0) & (rel <= nw)
    prompt, sample = [], []
    for g in range(N_DIL_GROUPS):
        bvec = _group_bias(rel_bias, g)
        prompt.append(jnp.where(band[None], bvec[jnp.clip(rel, 0, nw)].transpose(2, 0, 1), NEG_INF))
        sample.append(jnp.broadcast_to(bvec[::-1][:, :, None], (nw + 1, KV_HEADS, LANES)))
    return jnp.stack(prompt), jnp.stack(sample)


def _head_expand_matrix():
    k = jnp.arange(3 * LANES) % LANES
    j = jnp.arange(SSM_INNER) // SSM_HEAD_DIM
    return (k[:, None] == j[None, :]).astype(BF16)


def _pad_lanes(v):
    return jnp.pad(v.astype(F32), (0, LANES - v.shape[0]))[None]


def kernel(x_prompt, x_sample, state_ssm, state_conv, cache_kv, a_norm, a_w_in, a_conv_w, a_conv_b,
           a_dt_bias, a_A_log, a_D, a_gate_norm, a_w_out, rel_bias, kv_norm, w_kv, b_norm, b_w_in,
           b_w_out, final_norm):
    bp, sp, _ = x_prompt.shape
    db, steps, _ = x_sample.shape
    assert a_w_in.shape[0] == 1 and b_w_in.shape[0] == 1, "one A layer and one B layer"
    assert sp >= MAX_DISTANCE and sp % (N_WIN * DILATIONS[2]) == 0

    n_zx = SSM_INNER + CONV_DIM
    w_a = jnp.concatenate([a_w_in[0, :, :n_zx],
                           jnp.pad(a_w_in[0, :, n_zx:], ((0, 0), (0, LANES - SSM_HEADS)))],
                          axis=1).astype(BF16)
    dtb = _pad_lanes(a_dt_bias[0])
    a_pad = _pad_lanes(-jnp.exp(a_A_log[0].astype(F32)))
    d_exp = jnp.repeat(a_D[0].astype(F32), SSM_HEAD_DIM)[None]
    a_in = functools.partial(_a_in_proj, g=a_norm[0][None], w=w_a, cw=a_conv_w[0],
                             cb=a_conv_b[0][None], dtb=dtb, a_pad=a_pad)
    mid = functools.partial(_mid_proj, gg=a_gate_norm[0][None], wo=a_w_out[0].astype(BF16),
                            kvn=kv_norm[None], wkv=w_kv.astype(BF16), bn=b_norm[0][None],
                            wbi=b_w_in[0].astype(BF16))
    out = functools.partial(_out_proj, wo=b_w_out[0].astype(BF16), fn=final_norm[None])
    bias_prompt, bias_sample = _bias_tables(rel_bias)

    conv0 = jnp.zeros((bp, CONV_WIDTH - 1, CONV_DIM), F32)
    z_p, act_p, dt_p, _, conv_p = a_in(x_prompt, conv0, tm=256, shift=1)
    y_p, st_p = _ssd_prompt(act_p, dt_p, a_pad, d_exp, _head_expand_matrix())
    rows_p = bp * sp
    h1_p, kv_p, qg_p = mid(y_p.reshape(rows_p, SSM_INNER), z_p.reshape(rows_p, SSM_INNER),
                           x_prompt.reshape(rows_p, D_MODEL), tm=256)
    o_p = _attn_prompt(qg_p.reshape(bp, sp, -1), kv_p.reshape(bp, sp, -1), bias_prompt)
    y_prompt = out(o_p.reshape(rows_p, ATT_WIDTH), qg_p, h1_p, tm=512).reshape(bp, sp, D_MODEL)

    rows_s = steps * db
    x_s = x_sample.transpose(1, 0, 2).reshape(1, rows_s, D_MODEL)
    cprev_s = state_conv[0].astype(F32).transpose(1, 0, 2).reshape(1, (CONV_WIDTH - 1) * db, CONV_DIM)
    z_s, act_s, dt_s, da_s, conv_s = a_in(x_s, cprev_s, tm=rows_s, shift=db)
    xm = (act_s[0, :, :SSM_INNER].reshape(steps, db, SSM_HEADS, SSM_HEAD_DIM)
          .transpose(1, 3, 2, 0).reshape(db, SSM_HEAD_DIM, SSM_HEADS * steps))
    bc = act_s[0, :, SSM_INNER:].reshape(steps, db, -1).transpose(1, 0, 2)
    per_step = lambda v: (v[0, :, :SSM_HEADS].reshape(steps, db, SSM_HEADS)
                          .transpose(1, 2, 0).reshape(db, SSM_HEADS * steps))
    d_lane = jnp.repeat(a_D[0].astype(F32), steps)[None]
    ym, st_s = _ssd_sample(per_step(dt_s), per_step(da_s), state_ssm[0].astype(F32), xm, bc, d_lane,
                           steps=steps)
    y_s = (ym.reshape(db, SSM_HEAD_DIM, SSM_HEADS, steps).transpose(3, 0, 2, 1)
           .reshape(rows_s, SSM_INNER))
    h1_s, kv_s, qg_s = mid(y_s, z_s[0], x_s[0], tm=rows_s)
    kv_new = kv_s.reshape(steps, db, 2, KV_HEADS, ATT_HEAD_DIM).transpose(1, 0, 2, 3, 4)
    q_s = (qg_s[:, :Q_WIDTH].reshape(steps, db, N_DIL_GROUPS, KV_HEADS, ATT_HEAD_DIM)
           .transpose(1, 2, 0, 3, 4).reshape(db, N_DIL_GROUPS * steps, KV_HEADS, ATT_HEAD_DIM))
    o_s = _attn_sample(q_s, kv_new, cache_kv.astype(F32), bias_sample, steps=steps)
    y_s2 = out(o_s.transpose(1, 0, 2, 3).reshape(rows_s, ATT_WIDTH), qg_s, h1_s, tm=rows_s)
    y_sample = y_s2.reshape(steps, db, D_MODEL).transpose(1, 0, 2)

    n_keep = min(MAX_DISTANCE, sp)
    ssm_prompt = st_p.reshape(1, bp, SSM_HEADS, SSM_HEAD_DIM, SSM_STATE).astype(state_ssm.dtype)
    ssm_sample = st_s[None].astype(state_ssm.dtype)
    conv_prompt = conv_p[None].astype(state_conv.dtype)
    conv_sample = (conv_s.reshape(CONV_WIDTH - 1, db, CONV_DIM).transpose(1, 0, 2)[None]
                   .astype(state_conv.dtype))
    kv_prompt = (kv_p.reshape(bp, sp, 2, KV_HEADS, ATT_HEAD_DIM)[:, sp - n_keep:]
                 .astype(cache_kv.dtype))
    kv_sample = kv_new.astype(cache_kv.dtype)
    return (y_prompt, y_sample, ssm_prompt, ssm_sample, conv_prompt, conv_sample, kv_prompt, kv_sample)
```

```python
import functools
import math

import jax
import jax.numpy as jnp
from jax import lax
from jax.experimental import pallas as pl
from jax.experimental.pallas import tpu as pltpu

F32 = jnp.float32
BF16 = jnp.bfloat16

D_MODEL = 1024
SSM_INNER = 2048
SSM_HEADS = 32
SSM_HEAD_DIM = 64
SSM_GROUPS = 4
SSM_HEADS_PER_GROUP = SSM_HEADS // SSM_GROUPS
SSM_STATE = 128
CONV_WIDTH = 4
CONV_DIM = SSM_INNER + 2 * SSM_GROUPS * SSM_STATE
SSM_CHUNK = 128
ATT_HEAD_DIM = 128
KV_HEADS = 8
WINDOWS = (128, 512, 2048)
DILATIONS = (1, 4, 16)
N_DIL_GROUPS = 3
N_WIN = 128
Q_WIDTH = N_DIL_GROUPS * KV_HEADS * ATT_HEAD_DIM
ATT_WIDTH = KV_HEADS * ATT_HEAD_DIM
N_BUCKETS = 32
MAX_DISTANCE = max(WINDOWS)
NORM_EPS = 1e-5
NEG_INF = -1e30

LANES = 128
SUBLANES = 8
VMEM_LIMIT_BYTES = 56 * 1024 * 1024

assert all(w // d == N_WIN for w, d in zip(WINDOWS, DILATIONS))


def _round_up(x, m):
    return (x + m - 1) // m * m


def _dot(a, b):
    return jnp.dot(a, b, preferred_element_type=F32)


def _dot_nt(a, b):
    return lax.dot_general(a, b, (((1,), (1,)), ((), ())), preferred_element_type=F32)


def _dot_tn(a, b):
    return lax.dot_general(a, b, (((0,), (0,)), ((), ())), preferred_element_type=F32)


def _rms_scale(x):
    return lax.rsqrt(jnp.mean(x * x, axis=-1, keepdims=True) + NORM_EPS)


def _silu(x):
    return x * (1.0 / (1.0 + jnp.exp(-x)))


def _softplus(x):
    return jnp.maximum(x, 0.0) + jnp.log1p(jnp.exp(-jnp.abs(x)))


def _split3(v):
    h1 = v.astype(BF16)
    r1 = v - h1.astype(F32)
    h2 = r1.astype(BF16)
    h3 = (r1 - h2.astype(F32)).astype(BF16)
    return h1, h2, h3


def _const_spec(shape):
    nd = len(shape)
    return pl.BlockSpec(shape, lambda *_: (0,) * nd, pipeline_mode=pl.Buffered(1))


def _params(*sem):
    return pltpu.CompilerParams(dimension_semantics=sem, vmem_limit_bytes=VMEM_LIMIT_BYTES)


def _a_in_kernel(x_ref, g_ref, w_ref, cw_ref, cb_ref, dtb_ref, a_ref, cprev_ref,
                 z_ref, act_ref, dt_ref, da_ref, cstate_ref, xc_ref, *, shift):
    tm = x_ref.shape[1]
    hist = (CONV_WIDTH - 1) * shift
    pad = _round_up(hist, SUBLANES)

    x = x_ref[0]
    xn = (x * _rms_scale(x) * g_ref[...]).astype(BF16)
    z_ref[0] = _dot(xn, w_ref[:, :SSM_INNER])

    @pl.when(pl.program_id(1) == 0)
    def _():
        xc_ref[pad - hist:pad, :] = cprev_ref[0]

    xc_ref[pad:pad + tm, :] = _dot(xn, w_ref[:, SSM_INNER:SSM_INNER + CONV_DIM])
    conv = cb_ref[...]
    for k in range(CONV_WIDTH):
        conv = conv + xc_ref[pl.ds(pad - hist + k * shift, tm), :] * cw_ref[k:k + 1, :]
    act_ref[0] = _silu(conv)

    dt = _softplus(_dot(xn, w_ref[:, SSM_INNER + CONV_DIM:]) + dtb_ref[...])
    dt_ref[0] = dt
    da_ref[0] = jnp.exp(dt * a_ref[...])

    tail = xc_ref[pad + tm - hist:pad + tm, :]
    cstate_ref[0] = tail
    xc_ref[pad - hist:pad, :] = tail


def _a_in_proj(x, conv_prev, g, w, cw, cb, dtb, a_pad, *, tm, shift):
    nb, rows, _ = x.shape
    hist = (CONV_WIDTH - 1) * shift
    pad = _round_up(hist, SUBLANES)
    row_blk = lambda width: pl.BlockSpec((1, tm, width), lambda b, i: (b, i, 0))
    return pl.pallas_call(
        functools.partial(_a_in_kernel, shift=shift),
        grid=(nb, rows // tm),
        in_specs=[
            row_blk(D_MODEL),
            _const_spec((1, D_MODEL)),
            _const_spec(w.shape),
            _const_spec((CONV_WIDTH, CONV_DIM)),
            _const_spec((1, CONV_DIM)),
            _const_spec((1, LANES)),
            _const_spec((1, LANES)),
            pl.BlockSpec((1, hist, CONV_DIM), lambda b, i: (b, 0, 0)),
        ],
        out_specs=[
            row_blk(SSM_INNER), row_blk(CONV_DIM), row_blk(LANES), row_blk(LANES),
            pl.BlockSpec((1, hist, CONV_DIM), lambda b, i: (b, 0, 0)),
        ],
        out_shape=[
            jax.ShapeDtypeStruct((nb, rows, SSM_INNER), F32),
            jax.ShapeDtypeStruct((nb, rows, CONV_DIM), F32),
            jax.ShapeDtypeStruct((nb, rows, LANES), F32),
            jax.ShapeDtypeStruct((nb, rows, LANES), F32),
            jax.ShapeDtypeStruct((nb, hist, CONV_DIM), F32),
        ],
        scratch_shapes=[pltpu.VMEM((pad + tm, CONV_DIM), F32)],
        compiler_params=_params("arbitrary", "arbitrary"),
        name="a_in_proj",
    )(x, g, w, cw, cb, dtb, a_pad, conv_prev)


def _ssd_chunk_kernel(act_ref, dt_ref, a_ref, dexp_ref, e3_ref, y_ref, st_ref, sT_ref, ex_ref):
    T = SSM_CHUNK
    c = pl.program_id(1)

    @pl.when(c == 0)
    def _():
        sT_ref[...] = jnp.zeros_like(sT_ref)

    dt = dt_ref[0]
    adt = dt * a_ref[...]
    row = lax.broadcasted_iota(jnp.int32, (T, T), 0)
    col = lax.broadcasted_iota(jnp.int32, (T, T), 1)
    tri = row >= col
    ones_tri = jnp.where(tri, 1.0, 0.0).astype(BF16)
    p1, p2, p3 = _split3(adt)
    a_cs = _dot(ones_tri, p1) + _dot(ones_tri, p2) + _dot(ones_tri, p3)
    a_last = a_cs[T - 1:T, :]
    ea = jnp.exp(a_cs)
    wdt = jnp.exp(a_last - a_cs) * dt
    a_cs_t = a_cs.T
    dt_t = dt.T
    q1, q2, q3 = _split3(jnp.concatenate([ea, wdt], axis=0))
    ex_ref[...] = _dot(jnp.concatenate([q1, q2, q3], axis=1), e3_ref[...])

    lane = lax.broadcasted_iota(jnp.int32, (T, LANES), 1)
    gw = SSM_HEADS_PER_GROUP * SSM_HEAD_DIM
    for g in range(SSM_GROUPS):
        gs = slice(g * gw, (g + 1) * gw)
        b_off = SSM_INNER + g * SSM_STATE
        c_off = SSM_INNER + SSM_GROUPS * SSM_STATE + g * SSM_STATE
        bb = act_ref[0, :, b_off:b_off + SSM_STATE].astype(BF16)
        cc = act_ref[0, :, c_off:c_off + SSM_STATE].astype(BF16)
        cb = _dot_nt(cc, bb)
        s_old = sT_ref[:, gs]
        y_off = _dot(cc, s_old.astype(BF16)) * ex_ref[0:T, gs]
        xw = (act_ref[0, :, gs] * ex_ref[T:2 * T, gs]).astype(BF16)
        sT_ref[:, gs] = s_old * ex_ref[T - 1:T, gs] + _dot_tn(bb, xw)
        for jp in range(SSM_HEADS_PER_GROUP // 2):
            h0 = g * SSM_HEADS_PER_GROUP + 2 * jp
            ms = []
            for h in (h0, h0 + 1):
                seg = jnp.where(tri, jnp.exp(a_cs[:, h:h + 1] - a_cs_t[h:h + 1, :]), 0.0)
                ms.append((cb * seg * dt_t[h:h + 1, :]).astype(BF16))
            ls = slice(h0 * SSM_HEAD_DIM, h0 * SSM_HEAD_DIM + LANES)
            xp = act_ref[0, :, ls]
            rhs = jnp.concatenate([jnp.where(lane < SSM_HEAD_DIM, xp, 0.0).astype(BF16),
                                   jnp.where(lane >= SSM_HEAD_DIM, xp, 0.0).astype(BF16)], axis=0)
            y_diag = _dot(jnp.concatenate(ms, axis=1), rhs)
            y_ref[0, :, ls] = (y_diag + y_off[:, 2 * jp * SSM_HEAD_DIM:2 * jp * SSM_HEAD_DIM + LANES]
                               + xp * dexp_ref[:, ls])

    @pl.when(c == pl.num_programs(1) - 1)
    def _():
        for k in range(SSM_INNER // LANES):
            st_ref[0, k * LANES:(k + 1) * LANES, :] = sT_ref[:, k * LANES:(k + 1) * LANES].T


def _ssd_prompt(act, dt, a_pad, d_exp, e3):
    nb, seq, _ = act.shape
    T = SSM_CHUNK
    return pl.pallas_call(
        _ssd_chunk_kernel,
        grid=(nb, seq // T),
        in_specs=[
            pl.BlockSpec((1, T, CONV_DIM), lambda b, c: (b, c, 0)),
            pl.BlockSpec((1, T, LANES), lambda b, c: (b, c, 0)),
            _const_spec((1, LANES)),
            _const_spec((1, SSM_INNER)),
            _const_spec(e3.shape),
        ],
        out_specs=[
            pl.BlockSpec((1, T, SSM_INNER), lambda b, c: (b, c, 0)),
            pl.BlockSpec((1, SSM_INNER, SSM_STATE), lambda b, c: (b, 0, 0)),
        ],
        out_shape=[
            jax.ShapeDtypeStruct((nb, seq, SSM_INNER), F32),
            jax.ShapeDtypeStruct((nb, SSM_INNER, SSM_STATE), F32),
        ],
        scratch_shapes=[pltpu.VMEM((SSM_STATE, SSM_INNER), F32),
                        pltpu.VMEM((2 * T, SSM_INNER), F32)],
        compiler_params=_params("arbitrary", "arbitrary"),
        name="ssd_prompt",
    )(act, dt, a_pad, d_exp, e3)


def _ssd_step_kernel(da_ref, dt_ref, st_ref, xm_ref, bbig_ref, cbig_ref, a_ref, dl_ref,
                     y_ref, ns_ref, *, steps):
    n = LANES
    b = pl.program_id(0)
    shift = int(math.log2(steps))
    src = lax.broadcasted_iota(jnp.int32, (n, n), 0)
    dst = lax.broadcasted_iota(jnp.int32, (n, n), 1)
    same_head = lax.shift_right_logical(src, shift) == lax.shift_right_logical(dst, shift)
    causal = jnp.where(same_head, src, n) <= dst
    dt = jnp.broadcast_to(dt_ref[0], (SUBLANES, n))
    parts = _split3(dt * a_ref[...])
    upto = jnp.where(causal, 1.0, 0.0).astype(BF16)
    whole = jnp.where(same_head, 1.0, 0.0).astype(BF16)
    a_cs = _dot(parts[0], upto) + _dot(parts[1], upto) + _dot(parts[2], upto)
    a_tot = _dot(parts[0], whole) + _dot(parts[1], whole) + _dot(parts[2], whole)
    ea = jnp.exp(a_cs[0:1])
    wv = jnp.exp(a_tot - a_cs) * dt
    column = lambda v: jnp.broadcast_to(v[0:1], (n, n)).T
    bb = bbig_ref[0]
    cc = cbig_ref[0].astype(BF16)
    g = jnp.where(causal, _dot_nt(bb.astype(BF16), cc) * jnp.exp(a_cs[0:1] - column(a_cs)) * column(dt),
                  0.0)
    bw = (bb * column(wv)).astype(BF16)
    xm = xm_ref[0]
    y = _dot(xm.astype(BF16), g.astype(BF16)) + xm * dl_ref[...]
    lane_head = lax.shift_right_logical(lax.broadcasted_iota(jnp.int32, (SSM_HEAD_DIM, n), 1), shift)
    y_off = jnp.zeros((SSM_HEAD_DIM, n), F32)
    for h in range(SSM_HEADS):
        s0 = st_ref[0, h]
        mine = lane_head == h
        y_off = jnp.where(mine, _dot_nt(s0.astype(BF16), cc), y_off)
        decay = da_ref[b, h * steps]
        for t in range(1, steps):
            decay = decay * da_ref[b, h * steps + t]
        ns_ref[0, h] = s0 * decay + _dot(jnp.where(mine, xm, 0.0).astype(BF16), bw)
    y_ref[0] = y + y_off * ea


def _ssd_sample(da_s, dt_s, state, xm, bbig, cbig, a_lane, d_lane, *, steps):
    nb = state.shape[0]
    assert SSM_HEADS * steps == LANES and steps & (steps - 1) == 0
    st_spec = pl.BlockSpec((1, SSM_HEADS, SSM_HEAD_DIM, SSM_STATE), lambda b: (b, 0, 0, 0))
    xm_spec = pl.BlockSpec((1, SSM_HEAD_DIM, LANES), lambda b: (b, 0, 0))
    big_spec = pl.BlockSpec((1, LANES, SSM_STATE), lambda b: (b, 0, 0))
    lane_spec = pl.BlockSpec((1, LANES), lambda b: (0, 0))
    return pl.pallas_call(
        functools.partial(_ssd_step_kernel, steps=steps),
        grid=(nb,),
        in_specs=[pl.BlockSpec(memory_space=pltpu.SMEM),
                  pl.BlockSpec((1, 1, LANES), lambda b: (b, 0, 0)),
                  st_spec, xm_spec, big_spec, big_spec, lane_spec, lane_spec],
        out_specs=[xm_spec, st_spec],
        out_shape=[jax.ShapeDtypeStruct(xm.shape, F32), jax.ShapeDtypeStruct(state.shape, F32)],
        compiler_params=_params("arbitrary"),
        name="ssd_sample",
    )(da_s, dt_s, state, xm, bbig, cbig, a_lane, d_lane)


def _mid_kernel(y_ref, z_ref, h_ref, gg_ref, wo_ref, kvn_ref, wkv_ref, bn_ref, wbi_ref,
                h1_ref, kv_ref, qg_ref):
    yg = y_ref[...] * _silu(z_ref[...])
    yn = (yg * _rms_scale(yg) * gg_ref[...]).astype(BF16)
    h1 = h_ref[...] + _dot(yn, wo_ref[...])
    h1_ref[...] = h1
    hn = h1 * _rms_scale(h1)
    kv_ref[...] = _dot((hn * kvn_ref[...]).astype(BF16), wkv_ref[...])
    qg_ref[...] = _dot((hn * bn_ref[...]).astype(BF16), wbi_ref[...])


def _mid_proj(y, z, h, gg, wo, kvn, wkv, bn, wbi, *, tm):
    rows = y.shape[0]
    row_blk = lambda width: pl.BlockSpec((tm, width), lambda i: (i, 0))
    return pl.pallas_call(
        _mid_kernel,
        grid=(rows // tm,),
        in_specs=[row_blk(SSM_INNER), row_blk(SSM_INNER), row_blk(D_MODEL),
                  _const_spec((1, SSM_INNER)), _const_spec(wo.shape),
                  _const_spec((1, D_MODEL)), _const_spec(wkv.shape),
                  _const_spec((1, D_MODEL)), _const_spec(wbi.shape)],
        out_specs=[row_blk(D_MODEL), row_blk(2 * ATT_WIDTH), row_blk(Q_WIDTH + ATT_WIDTH)],
        out_shape=[jax.ShapeDtypeStruct((rows, D_MODEL), F32),
                   jax.ShapeDtypeStruct((rows, 2 * ATT_WIDTH), F32),
                   jax.ShapeDtypeStruct((rows, Q_WIDTH + ATT_WIDTH), F32)],
        compiler_params=_params("arbitrary"),
        name="mid_proj",
    )(y, z, h, gg, wo, kvn, wkv, bn, wbi)


def _attn_prompt_kernel(q0_ref, q1_ref, q2_ref, k_ref, v_ref, bias_ref, o_ref, og_ref, lse_ref):
    seq = k_ref.shape[1]
    nw = N_WIN
    scale = ATT_HEAD_DIM ** -0.5
    q_refs = (q0_ref, q1_ref, q2_ref)

    def rows(g, r, j):
        d = DILATIONS[g]
        return pl.ds(r + d * nw * j, nw, stride=d)

    def attend(g, segments):
        load = lambda ref, r, j: ref[0, rows(g, r, j), :]
        kc, vc, blocks = {}, {}, []
        for si, (r, n0, count) in enumerate(segments):
            traced = not isinstance(n0, int)
            assert traced or n0 == 0
            js = list(range(count)) + ([-1] if traced else [])
            for u in js:
                j = jnp.maximum(n0 - 1, 0) if u < 0 else n0 + u
                kc[si, u] = load(k_ref, r, j).astype(BF16)
                vc[si, u] = load(v_ref, r, j).astype(BF16)
            for u in range(count):
                prev = (si, u - 1) if (u > 0 or traced) else None
                table = jnp.where(n0 == 0, 1, 0) if (u == 0 and traced) else 0
                blocks.append((r, n0 + u, (si, u), prev, table))
        qs = [(load(q_refs[g], r, n) * scale).astype(BF16) for r, n, _, _, _ in blocks]
        s_cur = [_dot_nt(q, kc[cur]) + bias_ref[g, 0, 0, :, nw:] for q, (_, _, cur, _, _) in zip(qs, blocks)]
        s_prev = [None if prev is None else _dot_nt(q, kc[prev]) + bias_ref[g, table, 0, :, :nw]
                  for q, (_, _, _, prev, table) in zip(qs, blocks)]
        tops = [sc if sp is None else jnp.maximum(sc, sp) for sc, sp in zip(s_cur, s_prev)]
        ms = [jnp.max(t, axis=-1, keepdims=True) for t in tops]
        p_cur = [jnp.exp(sc - m) for sc, m in zip(s_cur, ms)]
        p_prev = [None if sp is None else jnp.exp(sp - m) for sp, m in zip(s_prev, ms)]
        ls = [jnp.sum(pc if pp is None else pc + pp, axis=-1, keepdims=True)
              for pc, pp in zip(p_cur, p_prev)]
        outs = []
        for (_, _, cur, prev, _), pc, pp in zip(blocks, p_cur, p_prev):
            o = _dot(pc.astype(BF16), vc[cur])
            outs.append(o if pp is None else o + _dot(pp.astype(BF16), vc[prev]))
        for (r, n, _, _, _), o, m, l in zip(blocks, outs, ms, ls):
            og_ref[g, rows(g, r, n), :] = o * (1.0 / l)
            lse_ref[g, rows(g, r, n), :] = jnp.broadcast_to(m + jnp.log(l), (nw, LANES))

    per_trip = 8
    for g in range(N_DIL_GROUPS):
        d = DILATIONS[g]
        n_blocks = seq // (d * nw)
        if n_blocks > per_trip:
            per_class = n_blocks // per_trip
            assert n_blocks % per_trip == 0

            def trip(i, carry, g=g, per_class=per_class):
                attend(g, [(lax.div(i, per_class), lax.rem(i, per_class) * per_trip, per_trip)])
                return carry

            lax.fori_loop(0, d * per_class, trip, 0)
        else:
            classes = per_trip // n_blocks
            assert per_trip % n_blocks == 0 and d % classes == 0

            def trip(i, carry, g=g, classes=classes, n_blocks=n_blocks):
                attend(g, [(i * classes + c, 0, n_blocks) for c in range(classes)])
                return carry

            lax.fori_loop(0, d // classes, trip, 0)

    tile = 2 * nw
    def combine(i, carry):
        rs = pl.ds(pl.multiple_of(i * tile, tile), tile)
        l0, l1, l2 = lse_ref[0, rs, :], lse_ref[1, rs, :], lse_ref[2, rs, :]
        mx = jnp.maximum(jnp.maximum(l0, l1), l2)
        e0, e1, e2 = jnp.exp(l0 - mx), jnp.exp(l1 - mx), jnp.exp(l2 - mx)
        inv = 1.0 / (e0 + e1 + e2)
        o_ref[0, rs, :] = (e0 * og_ref[0, rs, :] + e1 * og_ref[1, rs, :] + e2 * og_ref[2, rs, :]) * inv
        return carry

    lax.fori_loop(0, seq // tile, combine, 0)


def _attn_prompt(qg, kv, bias):
    nb, seq, _ = kv.shape
    col_blk = lambda off: pl.BlockSpec((1, seq, ATT_HEAD_DIM), lambda b, h: (b, 0, off + h))
    return pl.pallas_call(
        _attn_prompt_kernel,
        grid=(nb, KV_HEADS),
        in_specs=[col_blk(0), col_blk(KV_HEADS), col_blk(2 * KV_HEADS),
                  col_blk(0), col_blk(KV_HEADS),
                  pl.BlockSpec((N_DIL_GROUPS, 2, 1, N_WIN, 2 * N_WIN), lambda b, h: (0, 0, h, 0, 0))],
        out_specs=col_blk(0),
        out_shape=jax.ShapeDtypeStruct((nb, seq, ATT_WIDTH), F32),
        scratch_shapes=[pltpu.VMEM((N_DIL_GROUPS, seq, ATT_HEAD_DIM), F32),
                        pltpu.VMEM((N_DIL_GROUPS, seq, LANES), F32)],
        compiler_params=_params("arbitrary", "arbitrary"),
        name="attn_prompt",
    )(qg, qg, qg, kv, kv, bias)


def _attn_sample_kernel(q_ref, new_ref, c0_ref, c1_ref, c2_ref, brev_ref, o_ref, w0_ref, *, steps):
    nw = N_WIN
    scale = ATT_HEAD_DIM ** -0.5
    w0_ref[0:nw] = c0_ref[...]
    w0_ref[nw:nw + steps] = new_ref[0]
    for t in range(steps):
        kx, vx = new_ref[0, t, 0], new_ref[0, t, 1]
        parts = []
        for g in range(N_DIL_GROUPS):
            if g == 0:
                ks, vs = w0_ref[t:t + nw, 0], w0_ref[t:t + nw, 1]
            else:
                c_ref = c1_ref if g == 1 else c2_ref
                ks, vs = c_ref[:, t, 0], c_ref[:, t, 1]
            q = q_ref[0, g * steps + t] * scale
            s = jnp.sum(ks * q[None], axis=-1, keepdims=True) + brev_ref[g, 0:nw]
            sx = jnp.sum(kx * q, axis=-1, keepdims=True) + brev_ref[g, nw]
            m = jnp.maximum(jnp.max(s, axis=0), sx)
            p = jnp.exp(s - m[None])
            px = jnp.exp(sx - m)
            parts.append((m, jnp.sum(p, axis=0) + px, jnp.sum(p * vs, axis=0) + px * vx))
        mx = jnp.maximum(jnp.maximum(parts[0][0], parts[1][0]), parts[2][0])
        den = jnp.zeros_like(mx)
        num = jnp.zeros_like(mx)
        for m, l, acc in parts:
            e = jnp.exp(m - mx)
            den = den + e * l
            num = num + e * acc
        o_ref[0, t] = num * (1.0 / den)


def _attn_sample(q, kv_new, cache, brev, *, steps):
    nb, n_old = cache.shape[0], cache.shape[1]
    nw = N_WIN
    assert n_old == nw * DILATIONS[2] and steps <= DILATIONS[1]
    tail = (2, KV_HEADS, ATT_HEAD_DIM)
    c1 = cache.reshape(nb, n_old // DILATIONS[1], DILATIONS[1], *tail)
    c2 = cache.reshape(nb, n_old // DILATIONS[2], DILATIONS[2], *tail)
    z3 = (0, 0, 0)
    return pl.pallas_call(
        functools.partial(_attn_sample_kernel, steps=steps),
        grid=(nb,),
        in_specs=[
            pl.BlockSpec((1, N_DIL_GROUPS * steps, KV_HEADS, ATT_HEAD_DIM), lambda b: (b, 0, 0, 0)),
            pl.BlockSpec((1, steps) + tail, lambda b: (b, 0) + z3),
            pl.BlockSpec((None, nw) + tail, lambda b: (b, n_old // nw - 1) + z3),
            pl.BlockSpec((None, nw, steps) + tail,
                         lambda b: (b, n_old // (nw * DILATIONS[1]) - 1, 0) + z3),
            pl.BlockSpec((None, nw, steps) + tail, lambda b: (b, 0, 0) + z3),
            pl.BlockSpec(brev.shape, lambda b: (0, 0, 0, 0)),
        ],
        out_specs=pl.BlockSpec((1, steps, KV_HEADS, ATT_HEAD_DIM), lambda b: (b, 0, 0, 0)),
        out_shape=jax.ShapeDtypeStruct((nb, steps, KV_HEADS, ATT_HEAD_DIM), F32),
        scratch_shapes=[pltpu.VMEM((nw + SUBLANES,) + tail, F32)],
        compiler_params=_params("arbitrary"),
        name="attn_sample",
    )(q, kv_new, cache, c1, c2, brev)


def _out_kernel(o_ref, gate_ref, h_ref, wo_ref, fn_ref, y_ref):
    yb = (o_ref[...] * _silu(gate_ref[...])).astype(BF16)
    h2 = h_ref[...] + _dot(yb, wo_ref[...])
    y_ref[...] = h2 * _rms_scale(h2) * fn_ref[...]


def _out_proj(o, qg, h1, wo, fn, *, tm):
    rows = o.shape[0]
    row_blk = pl.BlockSpec((tm, D_MODEL), lambda i: (i, 0))
    return pl.pallas_call(
        _out_kernel,
        grid=(rows // tm,),
        in_specs=[row_blk, pl.BlockSpec((tm, ATT_WIDTH), lambda i: (i, Q_WIDTH // ATT_WIDTH)),
                  row_blk, _const_spec(wo.shape), _const_spec((1, D_MODEL))],
        out_specs=row_blk,
        out_shape=jax.ShapeDtypeStruct((rows, D_MODEL), F32),
        compiler_params=_params("arbitrary"),
        name="out_proj",
    )(o, qg, h1, wo, fn)


def _t5_bucket(dist):
    max_exact = N_BUCKETS // 2
    n = jnp.maximum(dist, 0)
    nf = jnp.maximum(n, 1).astype(F32)
    large = max_exact + (jnp.log(nf / max_exact) / math.log(MAX_DISTANCE / max_exact)
                         * (N_BUCKETS - max_exact)).astype(jnp.int32)
    large = jnp.minimum(large, N_BUCKETS - 1)
    return jnp.where(n < max_exact, n, large)


def _group_bias(rel_bias, g):
    dist = jnp.arange(N_WIN + 1, dtype=jnp.int32) * DILATIONS[g]
    cols = rel_bias[:, g * KV_HEADS:(g + 1) * KV_HEADS].astype(F32)
    return cols[_t5_bucket(dist)]


def _banded_bias(bvec):
    nw = N_WIN
    neg = jnp.full((KV_HEADS, nw - 1), NEG_INF, F32)
    e = jnp.concatenate([neg, bvec.T, neg], axis=1)
    f = jnp.concatenate([e[:, ::-1], jnp.full((KV_HEADS, 1), NEG_INF, F32)], axis=1)
    period = 3 * nw
    skew = jnp.tile(f, (1, nw + 1))[:, :nw * (period + 1)].reshape(KV_HEADS, nw, period + 1)
    return skew[:, ::-1, :2 * nw]


def _bias_tables(rel_bias):
    nw = N_WIN
    masked = jnp.full((KV_HEADS, nw, nw), NEG_INF, F32)
    prompt, sample = [], []
    for g in range(N_DIL_GROUPS):
        bvec = _group_bias(rel_bias, g)
        band = _banded_bias(bvec)
        first = jnp.concatenate([masked, band[:, :, nw:]], axis=2)
        prompt.append(jnp.stack([band, first]))
        sample.append(jnp.broadcast_to(bvec[::-1][:, :, None], (nw + 1, KV_HEADS, LANES)))
    return jnp.stack(prompt), jnp.stack(sample)


def _head_expand_matrix():
    k = jnp.arange(3 * LANES) % LANES
    j = jnp.arange(SSM_INNER) // SSM_HEAD_DIM
    return (k[:, None] == j[None, :]).astype(BF16)


def _pad_lanes(v):
    return jnp.pad(v.astype(F32), (0, LANES - v.shape[0]))[None]


def kernel(x_prompt, x_sample, state_ssm, state_conv, cache_kv, a_norm, a_w_in, a_conv_w, a_conv_b,
           a_dt_bias, a_A_log, a_D, a_gate_norm, a_w_out, rel_bias, kv_norm, w_kv, b_norm, b_w_in,
           b_w_out, final_norm):
    bp, sp, _ = x_prompt.shape
    db, steps, _ = x_sample.shape
    assert a_w_in.shape[0] == 1 and b_w_in.shape[0] == 1, "one A layer and one B layer"
    assert sp >= MAX_DISTANCE and sp % (N_WIN * DILATIONS[2]) == 0

    n_zx = SSM_INNER + CONV_DIM
    w_a = jnp.concatenate([a_w_in[0, :, :n_zx],
                           jnp.pad(a_w_in[0, :, n_zx:], ((0, 0), (0, LANES - SSM_HEADS)))],
                          axis=1).astype(BF16)
    dtb = _pad_lanes(a_dt_bias[0])
    a_pad = _pad_lanes(-jnp.exp(a_A_log[0].astype(F32)))
    d_exp = jnp.repeat(a_D[0].astype(F32), SSM_HEAD_DIM)[None]
    a_in = functools.partial(_a_in_proj, g=a_norm[0][None], w=w_a, cw=a_conv_w[0],
                             cb=a_conv_b[0][None], dtb=dtb, a_pad=a_pad)
    mid = functools.partial(_mid_proj, gg=a_gate_norm[0][None], wo=a_w_out[0].astype(BF16),
                            kvn=kv_norm[None], wkv=w_kv.astype(BF16), bn=b_norm[0][None],
                            wbi=b_w_in[0].astype(BF16))
    out = functools.partial(_out_proj, wo=b_w_out[0].astype(BF16), fn=final_norm[None])
    bias_prompt, bias_sample = _bias_tables(rel_bias)

    conv0 = jnp.zeros((bp, CONV_WIDTH - 1, CONV_DIM), F32)
    z_p, act_p, dt_p, _, conv_p = a_in(x_prompt, conv0, tm=256, shift=1)
    y_p, st_p = _ssd_prompt(act_p, dt_p, a_pad, d_exp, _head_expand_matrix())
    rows_p = bp * sp
    h1_p, kv_p, qg_p = mid(y_p.reshape(rows_p, SSM_INNER), z_p.reshape(rows_p, SSM_INNER),
                           x_prompt.reshape(rows_p, D_MODEL), tm=256)
    o_p = _attn_prompt(qg_p.reshape(bp, sp, -1), kv_p.reshape(bp, sp, -1), bias_prompt)
    y_prompt = out(o_p.reshape(rows_p, ATT_WIDTH), qg_p, h1_p, tm=512).reshape(bp, sp, D_MODEL)

    rows_s = steps * db
    x_s = x_sample.transpose(1, 0, 2).reshape(1, rows_s, D_MODEL)
    cprev_s = state_conv[0].astype(F32).transpose(1, 0, 2).reshape(1, (CONV_WIDTH - 1) * db, CONV_DIM)
    z_s, act_s, dt_s, da_s, conv_s = a_in(x_s, cprev_s, tm=rows_s, shift=db)
    xm = (act_s[0, :, :SSM_INNER].reshape(steps, db, SSM_HEADS, SSM_HEAD_DIM)
          .transpose(1, 3, 2, 0).reshape(db, SSM_HEAD_DIM, SSM_HEADS * steps))
    per_column = lambda v: jnp.broadcast_to(
        v.reshape(steps, db, SSM_GROUPS, 1, SSM_STATE).transpose(1, 2, 3, 0, 4),
        (db, SSM_GROUPS, SSM_HEADS_PER_GROUP, steps, SSM_STATE)).reshape(db, SSM_HEADS * steps, SSM_STATE)
    gn = SSM_GROUPS * SSM_STATE
    bbig = per_column(act_s[0, :, SSM_INNER:SSM_INNER + gn])
    cbig = per_column(act_s[0, :, SSM_INNER + gn:])
    per_step = lambda v: (v[0, :, :SSM_HEADS].reshape(steps, db, SSM_HEADS)
                          .transpose(1, 2, 0).reshape(db, SSM_HEADS * steps))
    a_lane = jnp.repeat(-jnp.exp(a_A_log[0].astype(F32)), steps)[None]
    d_lane = jnp.repeat(a_D[0].astype(F32), steps)[None]
    ym, st_s = _ssd_sample(per_step(da_s), per_step(dt_s)[:, None, :], state_ssm[0].astype(F32), xm,
                           bbig, cbig, a_lane, d_lane, steps=steps)
    y_s = (ym.reshape(db, SSM_HEAD_DIM, SSM_HEADS, steps).transpose(3, 0, 2, 1)
           .reshape(rows_s, SSM_INNER))
    h1_s, kv_s, qg_s = mid(y_s, z_s[0], x_s[0], tm=rows_s)
    kv_new = kv_s.reshape(steps, db, 2, KV_HEADS, ATT_HEAD_DIM).transpose(1, 0, 2, 3, 4)
    q_s = (qg_s[:, :Q_WIDTH].reshape(steps, db, N_DIL_GROUPS, KV_HEADS, ATT_HEAD_DIM)
           .transpose(1, 2, 0, 3, 4).reshape(db, N_DIL_GROUPS * steps, KV_HEADS, ATT_HEAD_DIM))
    o_s = _attn_sample(q_s, kv_new, cache_kv.astype(F32), bias_sample, steps=steps)
    y_s2 = out(o_s.transpose(1, 0, 2, 3).reshape(rows_s, ATT_WIDTH), qg_s, h1_s, tm=rows_s)
    y_sample = y_s2.reshape(steps, db, D_MODEL).transpose(1, 0, 2)

    n_keep = min(MAX_DISTANCE, sp)
    ssm_prompt = st_p.reshape(1, bp, SSM_HEADS, SSM_HEAD_DIM, SSM_STATE).astype(state_ssm.dtype)
    ssm_sample = st_s[None].astype(state_ssm.dtype)
    conv_prompt = conv_p[None].astype(state_conv.dtype)
    conv_sample = (conv_s.reshape(CONV_WIDTH - 1, db, CONV_DIM).transpose(1, 0, 2)[None]
                   .astype(state_conv.dtype))
    kv_prompt = (kv_p.reshape(bp, sp, 2, KV_HEADS, ATT_HEAD_DIM)[:, sp - n_keep:]
                 .astype(cache_kv.dtype))
    kv_sample = kv_new.astype(cache_kv.dtype)
    return (y_prompt, y_sample, ssm_prompt, ssm_sample, conv_prompt, conv_sample, kv_prompt, kv_sample)
```

```python
import functools
import math

import jax
import jax.numpy as jnp
from jax import lax
from jax.experimental import pallas as pl
from jax.experimental.pallas import tpu as pltpu

F32 = jnp.float32
BF16 = jnp.bfloat16

D_MODEL = 1024
SSM_INNER = 2048
SSM_HEADS = 32
SSM_HEAD_DIM = 64
SSM_GROUPS = 4
SSM_HEADS_PER_GROUP = SSM_HEADS // SSM_GROUPS
SSM_STATE = 128
CONV_WIDTH = 4
CONV_DIM = SSM_INNER + 2 * SSM_GROUPS * SSM_STATE
SSM_CHUNK = 128
ATT_HEAD_DIM = 128
KV_HEADS = 8
WINDOWS = (128, 512, 2048)
DILATIONS = (1, 4, 16)
N_DIL_GROUPS = 3
N_WIN = 128
Q_WIDTH = N_DIL_GROUPS * KV_HEADS * ATT_HEAD_DIM
ATT_WIDTH = KV_HEADS * ATT_HEAD_DIM
N_BUCKETS = 32
MAX_DISTANCE = max(WINDOWS)
NORM_EPS = 1e-5
NEG_INF = -1e30

LANES = 128
SUBLANES = 8
COL_CHUNK = 512
ROW_TILE = 512
VMEM_LIMIT_BYTES = 56 * 1024 * 1024

assert all(w // d == N_WIN for w, d in zip(WINDOWS, DILATIONS))


def _round_up(x, m):
    return (x + m - 1) // m * m


def _dot(a, b):
    return jnp.dot(a, b, preferred_element_type=F32)


def _dot_nt(a, b):
    return lax.dot_general(a, b, (((1,), (1,)), ((), ())), preferred_element_type=F32)


def _dot_tn(a, b):
    return lax.dot_general(a, b, (((0,), (0,)), ((), ())), preferred_element_type=F32)


def _rms_scale(x):
    return lax.rsqrt(jnp.mean(x * x, axis=-1, keepdims=True) + NORM_EPS)


def _silu(x):
    return x * (1.0 / (1.0 + jnp.exp(-x)))


def _softplus(x):
    return jnp.maximum(x, 0.0) + jnp.log1p(jnp.exp(-jnp.abs(x)))


def _split3(v):
    h1 = v.astype(BF16)
    r1 = v - h1.astype(F32)
    h2 = r1.astype(BF16)
    h3 = (r1 - h2.astype(F32)).astype(BF16)
    return h1, h2, h3


def _const_spec(shape):
    nd = len(shape)
    return pl.BlockSpec(shape, lambda *_: (0,) * nd, pipeline_mode=pl.Buffered(1))


def _params(*sem):
    return pltpu.CompilerParams(dimension_semantics=sem, vmem_limit_bytes=VMEM_LIMIT_BYTES)


def _a_in_kernel(x_ref, g_ref, w_ref, cw_ref, cb_ref, dtb_ref, a_ref, cprev_ref,
                 z_ref, act_ref, dt_ref, da_ref, cstate_ref, xc_ref, *, shift):
    tm = x_ref.shape[1]
    hist = (CONV_WIDTH - 1) * shift
    pad = _round_up(hist, SUBLANES)

    x = x_ref[0]
    xn = (x * _rms_scale(x) * g_ref[...]).astype(BF16)

    @pl.when(pl.program_id(1) == 0)
    def _():
        xc_ref[pad - hist:pad, :] = cprev_ref[0]

    for j in range(CONV_DIM // COL_CHUNK):
        cs = slice(j * COL_CHUNK, (j + 1) * COL_CHUNK)
        xc_ref[pad:pad + tm, cs] = _dot(xn, w_ref[:, SSM_INNER + j * COL_CHUNK:SSM_INNER + (j + 1) * COL_CHUNK])
        conv = cb_ref[:, cs]
        for k in reversed(range(CONV_WIDTH)):
            conv = conv + xc_ref[pl.ds(pad - hist + k * shift, tm), cs] * cw_ref[k:k + 1, cs]
        act_ref[0, :, cs] = _silu(conv).astype(act_ref.dtype)
        if j < SSM_INNER // COL_CHUNK:
            z_ref[0, :, cs] = _dot(xn, w_ref[:, cs]).astype(z_ref.dtype)

    dt = _softplus(_dot(xn, w_ref[:, SSM_INNER + CONV_DIM:]) + dtb_ref[...])
    dt_ref[0] = dt
    da_ref[0] = jnp.exp(dt * a_ref[...])

    tail = xc_ref[pad + tm - hist:pad + tm, :]
    cstate_ref[0] = tail
    xc_ref[pad - hist:pad, :] = tail


def _a_in_proj(x, conv_prev, g, w, cw, cb, dtb, a_pad, *, tm, shift):
    nb, rows, _ = x.shape
    hist = (CONV_WIDTH - 1) * shift
    pad = _round_up(hist, SUBLANES)
    row_blk = lambda width: pl.BlockSpec((1, tm, width), lambda b, i: (b, i, 0))
    return pl.pallas_call(
        functools.partial(_a_in_kernel, shift=shift),
        grid=(nb, rows // tm),
        in_specs=[
            row_blk(D_MODEL),
            _const_spec((1, D_MODEL)),
            _const_spec(w.shape),
            _const_spec((CONV_WIDTH, CONV_DIM)),
            _const_spec((1, CONV_DIM)),
            _const_spec((1, LANES)),
            _const_spec((1, LANES)),
            pl.BlockSpec((1, hist, CONV_DIM), lambda b, i: (b, 0, 0)),
        ],
        out_specs=[
            row_blk(SSM_INNER), row_blk(CONV_DIM), row_blk(LANES), row_blk(LANES),
            pl.BlockSpec((1, hist, CONV_DIM), lambda b, i: (b, 0, 0)),
        ],
        out_shape=[
            jax.ShapeDtypeStruct((nb, rows, SSM_INNER), BF16),
            jax.ShapeDtypeStruct((nb, rows, CONV_DIM), BF16),
            jax.ShapeDtypeStruct((nb, rows, LANES), F32),
            jax.ShapeDtypeStruct((nb, rows, LANES), F32),
            jax.ShapeDtypeStruct((nb, hist, CONV_DIM), F32),
        ],
        scratch_shapes=[pltpu.VMEM((pad + tm, CONV_DIM), F32)],
        compiler_params=_params("arbitrary", "arbitrary"),
        name="a_in_proj",
    )(x, g, w, cw, cb, dtb, a_pad, conv_prev)


def _a_out(y_ref, z_ref, x, gg_ref, wo_ref):
    acc = jnp.zeros(x.shape, F32)
    ss = jnp.zeros((x.shape[0], 1), F32)
    for j in range(SSM_INNER // COL_CHUNK):
        cs = slice(j * COL_CHUNK, (j + 1) * COL_CHUNK)
        yg = y_ref[:, cs].astype(F32) * _silu(z_ref[:, cs].astype(F32))
        ss = ss + jnp.sum(yg * yg, axis=-1, keepdims=True)
        acc = acc + _dot((yg * gg_ref[:, cs]).astype(BF16), wo_ref[cs, :])
    return x + acc * lax.rsqrt(ss * (1.0 / SSM_INNER) + NORM_EPS)


def _ssd_prompt_kernel(act_ref, dt_ref, z_ref, x_ref, a_ref, dexp_ref, expand_ref, gg_ref, wo_ref,
                       h1_ref, st_ref, sT_ref, ex_ref, y_ref):
    T = SSM_CHUNK
    step = pl.program_id(1)

    @pl.when(step == 0)
    def _():
        sT_ref[...] = jnp.zeros_like(sT_ref)

    row = lax.broadcasted_iota(jnp.int32, (T, T), 0)
    col = lax.broadcasted_iota(jnp.int32, (T, T), 1)
    tri = row >= col
    ones_tri = jnp.where(tri, 1.0, 0.0).astype(BF16)
    lane = lax.broadcasted_iota(jnp.int32, (T, LANES), 1)
    gw = SSM_HEADS_PER_GROUP * SSM_HEAD_DIM

    def chunk(ci, carry):
        rows = pl.ds(pl.multiple_of(ci * T, T), T)
        dt = dt_ref[0, rows, :]
        p1, p2, p3 = _split3(dt * a_ref[...])
        a_cs = _dot(ones_tri, p1) + _dot(ones_tri, p2) + _dot(ones_tri, p3)
        ea = jnp.exp(a_cs)
        wdt = jnp.exp(a_cs[T - 1:T, :] - a_cs) * dt
        a_cs_t = a_cs.T
        dt_t = dt.T
        q1, q2, _ = _split3(jnp.concatenate([ea, wdt], axis=0))
        ex_ref[...] = _dot(jnp.concatenate([q1, q2], axis=1), expand_ref[...])
        for g in range(SSM_GROUPS):
            gs = slice(g * gw, (g + 1) * gw)
            b_off = SSM_INNER + g * SSM_STATE
            c_off = SSM_INNER + SSM_GROUPS * SSM_STATE + g * SSM_STATE
            bb = act_ref[0, rows, b_off:b_off + SSM_STATE]
            cc = act_ref[0, rows, c_off:c_off + SSM_STATE]
            cb = _dot_nt(cc, bb)
            s_old = sT_ref[:, gs]
            y_off = _dot(cc, s_old.astype(BF16)) * ex_ref[0:T, gs]
            xw = (act_ref[0, rows, gs].astype(F32) * ex_ref[T:2 * T, gs]).astype(BF16)
            sT_ref[:, gs] = s_old * ex_ref[T - 1:T, gs] + _dot_tn(bb, xw)
            for jp in range(SSM_HEADS_PER_GROUP // 2):
                h0 = g * SSM_HEADS_PER_GROUP + 2 * jp
                ms = []
                for h in (h0, h0 + 1):
                    seg = jnp.where(tri, jnp.exp(a_cs[:, h:h + 1] - a_cs_t[h:h + 1, :]), 0.0)
                    ms.append((cb * seg * dt_t[h:h + 1, :]).astype(BF16))
                ls = slice(h0 * SSM_HEAD_DIM, h0 * SSM_HEAD_DIM + LANES)
                xp = act_ref[0, rows, ls].astype(F32)
                rhs = jnp.concatenate([jnp.where(lane < SSM_HEAD_DIM, xp, 0.0).astype(BF16),
                                       jnp.where(lane >= SSM_HEAD_DIM, xp, 0.0).astype(BF16)], axis=0)
                y_diag = _dot(jnp.concatenate(ms, axis=1), rhs)
                y_ref[rows, ls] = (y_diag + y_off[:, 2 * jp * SSM_HEAD_DIM:2 * jp * SSM_HEAD_DIM + LANES]
                                   + xp * dexp_ref[:, ls])
        return carry

    lax.fori_loop(0, act_ref.shape[1] // T, chunk, 0)
    h1_ref[0] = _a_out(y_ref, z_ref.at[0], x_ref[0], gg_ref, wo_ref)

    @pl.when(step == pl.num_programs(1) - 1)
    def _():
        for k in range(SSM_INNER // LANES):
            st_ref[0, k * LANES:(k + 1) * LANES, :] = sT_ref[:, k * LANES:(k + 1) * LANES].T


def _ssd_prompt(act, dt, z, x, a_pad, d_exp, expand, gg, wo, *, tm):
    nb, seq, _ = act.shape
    T = SSM_CHUNK
    row_blk = lambda width: pl.BlockSpec((1, tm, width), lambda b, c: (b, c, 0))
    return pl.pallas_call(
        _ssd_prompt_kernel,
        grid=(nb, seq // tm),
        in_specs=[row_blk(CONV_DIM), row_blk(LANES), row_blk(SSM_INNER), row_blk(D_MODEL),
                  _const_spec((1, LANES)), _const_spec((1, SSM_INNER)), _const_spec(expand.shape),
                  _const_spec((1, SSM_INNER)), _const_spec(wo.shape)],
        out_specs=[row_blk(D_MODEL),
                   pl.BlockSpec((1, SSM_INNER, SSM_STATE), lambda b, c: (b, 0, 0))],
        out_shape=[jax.ShapeDtypeStruct((nb, seq, D_MODEL), F32),
                   jax.ShapeDtypeStruct((nb, SSM_INNER, SSM_STATE), F32)],
        scratch_shapes=[pltpu.VMEM((SSM_STATE, SSM_INNER), F32),
                        pltpu.VMEM((2 * T, SSM_INNER), F32),
                        pltpu.VMEM((tm, SSM_INNER), F32)],
        compiler_params=_params("arbitrary", "arbitrary"),
        name="ssd_prompt",
    )(act, dt, z, x, a_pad, d_exp, expand, gg, wo)


def _ssd_step_kernel(da_ref, dt_ref, st_ref, xm_ref, bbig_ref, cbig_ref, a_ref, dl_ref,
                     y_ref, ns_ref, *, steps):
    n = LANES
    b = pl.program_id(0)
    shift = int(math.log2(steps))
    src = lax.broadcasted_iota(jnp.int32, (n, n), 0)
    dst = lax.broadcasted_iota(jnp.int32, (n, n), 1)
    same_head = lax.shift_right_logical(src, shift) == lax.shift_right_logical(dst, shift)
    causal = jnp.where(same_head, src, n) <= dst
    dt = jnp.broadcast_to(dt_ref[0], (SUBLANES, n))
    parts = _split3(dt * a_ref[...])
    upto = jnp.where(causal, 1.0, 0.0).astype(BF16)
    whole = jnp.where(same_head, 1.0, 0.0).astype(BF16)
    a_cs = _dot(parts[0], upto) + _dot(parts[1], upto) + _dot(parts[2], upto)
    a_tot = _dot(parts[0], whole) + _dot(parts[1], whole) + _dot(parts[2], whole)
    ea = jnp.exp(a_cs[0:1])
    wv = jnp.exp(a_tot - a_cs) * dt
    column = lambda v: jnp.broadcast_to(v[0:1], (n, n)).T
    bb = bbig_ref[0]
    cc = cbig_ref[0].astype(BF16)
    g = jnp.where(causal, _dot_nt(bb.astype(BF16), cc) * jnp.exp(a_cs[0:1] - column(a_cs)) * column(dt),
                  0.0)
    bw = (bb * column(wv)).astype(BF16)
    xm = xm_ref[0]
    y = _dot(xm.astype(BF16), g.astype(BF16)) + xm * dl_ref[...]
    lane_head = lax.shift_right_logical(lax.broadcasted_iota(jnp.int32, (SSM_HEAD_DIM, n), 1), shift)
    y_off = jnp.zeros((SSM_HEAD_DIM, n), F32)
    for h in range(SSM_HEADS):
        s0 = st_ref[0, h]
        mine = lane_head == h
        y_off = jnp.where(mine, _dot_nt(s0.astype(BF16), cc), y_off)
        decay = da_ref[b, h * steps]
        for t in range(1, steps):
            decay = decay * da_ref[b, h * steps + t]
        ns_ref[0, h] = s0 * decay + _dot(jnp.where(mine, xm, 0.0).astype(BF16), bw)
    y_ref[0] = y + y_off * ea


def _ssd_sample(da_s, dt_s, state, xm, bbig, cbig, a_lane, d_lane, *, steps):
    nb = state.shape[0]
    assert SSM_HEADS * steps == LANES and steps & (steps - 1) == 0
    st_spec = pl.BlockSpec((1, SSM_HEADS, SSM_HEAD_DIM, SSM_STATE), lambda b: (b, 0, 0, 0))
    xm_spec = pl.BlockSpec((1, SSM_HEAD_DIM, LANES), lambda b: (b, 0, 0))
    big_spec = pl.BlockSpec((1, LANES, SSM_STATE), lambda b: (b, 0, 0))
    lane_spec = pl.BlockSpec((1, LANES), lambda b: (0, 0))
    return pl.pallas_call(
        functools.partial(_ssd_step_kernel, steps=steps),
        grid=(nb,),
        in_specs=[pl.BlockSpec(memory_space=pltpu.SMEM),
                  pl.BlockSpec((1, 1, LANES), lambda b: (b, 0, 0)),
                  st_spec, xm_spec, big_spec, big_spec, lane_spec, lane_spec],
        out_specs=[xm_spec, st_spec],
        out_shape=[jax.ShapeDtypeStruct(xm.shape, F32), jax.ShapeDtypeStruct(state.shape, F32)],
        compiler_params=_params("arbitrary"),
        name="ssd_sample",
    )(da_s, dt_s, state, xm, bbig, cbig, a_lane, d_lane)


def _a_out_kernel(y_ref, z_ref, x_ref, gg_ref, wo_ref, h1_ref):
    h1_ref[...] = _a_out(y_ref, z_ref, x_ref[...], gg_ref, wo_ref)


def _a_out_proj(y, z, x, gg, wo):
    rows = y.shape[0]
    return pl.pallas_call(
        _a_out_kernel,
        grid=(1,),
        in_specs=[_const_spec(y.shape), _const_spec(z.shape), _const_spec(x.shape),
                  _const_spec((1, SSM_INNER)), _const_spec(wo.shape)],
        out_specs=pl.BlockSpec((rows, D_MODEL), lambda i: (0, 0)),
        out_shape=jax.ShapeDtypeStruct((rows, D_MODEL), F32),
        compiler_params=_params("arbitrary"),
        name="a_out_proj",
    )(y, z, x, gg, wo)


def _b_in_kernel(h_ref, kvn_ref, wkv_ref, bn_ref, wbi_ref, kv_ref, kvt_ref, q_ref, gate_ref):
    h1 = h_ref[...]
    hn = h1 * _rms_scale(h1)
    hk = (hn * kvn_ref[...]).astype(BF16)
    hb = (hn * bn_ref[...]).astype(BF16)
    for j in range(2 * ATT_WIDTH // COL_CHUNK):
        cs = slice(j * COL_CHUNK, (j + 1) * COL_CHUNK)
        kv = _dot(hk, wkv_ref[:, cs])
        kv_ref[:, cs] = kv
        kvt_ref[:, cs] = kv
    for j in range(Q_WIDTH // COL_CHUNK):
        cs = slice(j * COL_CHUNK, (j + 1) * COL_CHUNK)
        q_ref[:, cs] = _dot(hb, wbi_ref[:, cs])
    for j in range(ATT_WIDTH // COL_CHUNK):
        cs = slice(j * COL_CHUNK, (j + 1) * COL_CHUNK)
        gate_ref[:, cs] = _dot(hb, wbi_ref[:, Q_WIDTH + j * COL_CHUNK:Q_WIDTH + (j + 1) * COL_CHUNK]
                               ).astype(gate_ref.dtype)


def _b_in_proj(h1, kvn, wkv, bn, wbi, *, tm, seq_tiles, keep_tiles):
    rows = h1.shape[0]
    skip = seq_tiles - keep_tiles
    row_blk = lambda width: pl.BlockSpec((tm, width), lambda i: (i, 0))
    tail_blk = pl.BlockSpec(
        (tm, 2 * ATT_WIDTH),
        lambda i: ((i // seq_tiles) * keep_tiles + jnp.maximum(i % seq_tiles - skip, 0), 0))
    return pl.pallas_call(
        _b_in_kernel,
        grid=(rows // tm,),
        in_specs=[row_blk(D_MODEL), _const_spec((1, D_MODEL)), _const_spec(wkv.shape),
                  _const_spec((1, D_MODEL)), _const_spec(wbi.shape)],
        out_specs=[row_blk(2 * ATT_WIDTH), tail_blk, row_blk(Q_WIDTH), row_blk(ATT_WIDTH)],
        out_shape=[jax.ShapeDtypeStruct((rows, 2 * ATT_WIDTH), F32),
                   jax.ShapeDtypeStruct((rows // seq_tiles * keep_tiles, 2 * ATT_WIDTH), F32),
                   jax.ShapeDtypeStruct((rows, Q_WIDTH), F32),
                   jax.ShapeDtypeStruct((rows, ATT_WIDTH), BF16)],
        compiler_params=_params("arbitrary"),
        name="b_in_proj",
    )(h1, kvn, wkv, bn, wbi)


def _attn_prompt_kernel(q0_ref, q1_ref, q2_ref, k_ref, v_ref, bias_ref, o_ref, og_ref, lse_ref):
    seq = k_ref.shape[1]
    nw = N_WIN
    scale = ATT_HEAD_DIM ** -0.5
    q_refs = (q0_ref, q1_ref, q2_ref)

    def rows(g, r, j):
        d = DILATIONS[g]
        return pl.ds(r + d * nw * j, nw, stride=d)

    def attend(g, segments):
        load = lambda ref, r, j: ref[0, rows(g, r, j), :]
        kc, vc, blocks = {}, {}, []
        for si, (r, n0, count) in enumerate(segments):
            traced = not isinstance(n0, int)
            assert traced or n0 == 0
            js = list(range(count)) + ([-1] if traced else [])
            for u in js:
                j = jnp.maximum(n0 - 1, 0) if u < 0 else n0 + u
                kc[si, u] = load(k_ref, r, j).astype(BF16)
                vc[si, u] = load(v_ref, r, j).astype(BF16)
            for u in range(count):
                prev = (si, u - 1) if (u > 0 or traced) else None
                table = jnp.where(n0 == 0, 1, 0) if (u == 0 and traced) else 0
                blocks.append((r, n0 + u, (si, u), prev, table))
        qs = [(load(q_refs[g], r, n) * scale).astype(BF16) for r, n, _, _, _ in blocks]
        s_cur = [_dot_nt(q, kc[cur]) + bias_ref[g, 0, 0, :, nw:] for q, (_, _, cur, _, _) in zip(qs, blocks)]
        s_prev = [None if prev is None else _dot_nt(q, kc[prev]) + bias_ref[g, table, 0, :, :nw]
                  for q, (_, _, _, prev, table) in zip(qs, blocks)]
        tops = [sc if sp is None else jnp.maximum(sc, sp) for sc, sp in zip(s_cur, s_prev)]
        ms = [jnp.max(t, axis=-1, keepdims=True) for t in tops]
        p_cur = [jnp.exp(sc - m) for sc, m in zip(s_cur, ms)]
        p_prev = [None if sp is None else jnp.exp(sp - m) for sp, m in zip(s_prev, ms)]
        ls = [jnp.sum(pc if pp is None else pc + pp, axis=-1, keepdims=True)
              for pc, pp in zip(p_cur, p_prev)]
        outs = []
        for (_, _, cur, prev, _), pc, pp in zip(blocks, p_cur, p_prev):
            o = _dot(pc.astype(BF16), vc[cur])
            outs.append(o if pp is None else o + _dot(pp.astype(BF16), vc[prev]))
        for (r, n, _, _, _), o, m, l in zip(blocks, outs, ms, ls):
            og_ref[g, rows(g, r, n), :] = o * (1.0 / l)
            lse_ref[g, rows(g, r, n), :] = jnp.broadcast_to(m + jnp.log(l), (nw, LANES))

    per_trip = 8
    for g in range(N_DIL_GROUPS):
        d = DILATIONS[g]
        n_blocks = seq // (d * nw)
        if n_blocks > per_trip:
            per_class = n_blocks // per_trip
            assert n_blocks % per_trip == 0

            def trip(i, carry, g=g, per_class=per_class):
                attend(g, [(lax.div(i, per_class), lax.rem(i, per_class) * per_trip, per_trip)])
                return carry

            lax.fori_loop(0, d * per_class, trip, 0)
        else:
            classes = per_trip // n_blocks
            assert per_trip % n_blocks == 0 and d % classes == 0

            def trip(i, carry, g=g, classes=classes, n_blocks=n_blocks):
                attend(g, [(i * classes + c, 0, n_blocks) for c in range(classes)])
                return carry

            lax.fori_loop(0, d // classes, trip, 0)

    tile = 2 * nw
    def combine(i, carry):
        rs = pl.ds(pl.multiple_of(i * tile, tile), tile)
        l0, l1, l2 = lse_ref[0, rs, :], lse_ref[1, rs, :], lse_ref[2, rs, :]
        mx = jnp.maximum(jnp.maximum(l0, l1), l2)
        e0, e1, e2 = jnp.exp(l0 - mx), jnp.exp(l1 - mx), jnp.exp(l2 - mx)
        inv = 1.0 / (e0 + e1 + e2)
        o_ref[0, rs, :] = (e0 * og_ref[0, rs, :] + e1 * og_ref[1, rs, :] + e2 * og_ref[2, rs, :]) * inv
        return carry

    lax.fori_loop(0, seq // tile, combine, 0)


def _attn_prompt(q, kv, bias):
    nb, seq, _ = kv.shape
    col_blk = lambda off: pl.BlockSpec((1, seq, ATT_HEAD_DIM), lambda b, h: (b, 0, off + h))
    return pl.pallas_call(
        _attn_prompt_kernel,
        grid=(nb, KV_HEADS),
        in_specs=[col_blk(0), col_blk(KV_HEADS), col_blk(2 * KV_HEADS),
                  col_blk(0), col_blk(KV_HEADS),
                  pl.BlockSpec((N_DIL_GROUPS, 2, 1, N_WIN, 2 * N_WIN), lambda b, h: (0, 0, h, 0, 0))],
        out_specs=col_blk(0),
        out_shape=jax.ShapeDtypeStruct((nb, seq, ATT_WIDTH), F32),
        scratch_shapes=[pltpu.VMEM((N_DIL_GROUPS, seq, ATT_HEAD_DIM), F32),
                        pltpu.VMEM((N_DIL_GROUPS, seq, LANES), F32)],
        compiler_params=_params("arbitrary", "arbitrary"),
        name="attn_prompt",
    )(q, q, q, kv, kv, bias)


def _attn_sample_kernel(q_ref, new_ref, c1_ref, c2_ref, brev_ref, o_ref, w0_ref, *, steps):
    nw = N_WIN
    scale = ATT_HEAD_DIM ** -0.5
    d1 = DILATIONS[1]
    w0_ref[0:nw] = c1_ref[nw - nw // d1:].reshape((nw,) + w0_ref.shape[1:])
    w0_ref[nw:nw + steps] = new_ref[0]
    for t in range(steps):
        kx, vx = new_ref[0, t, 0], new_ref[0, t, 1]
        parts = []
        for g in range(N_DIL_GROUPS):
            if g == 0:
                ks, vs = w0_ref[t:t + nw, 0], w0_ref[t:t + nw, 1]
            else:
                c_ref = c1_ref if g == 1 else c2_ref
                ks, vs = c_ref[:, t, 0], c_ref[:, t, 1]
            q = q_ref[0, g * steps + t] * scale
            s = jnp.sum(ks * q[None], axis=-1, keepdims=True) + brev_ref[g, 0:nw]
            sx = jnp.sum(kx * q, axis=-1, keepdims=True) + brev_ref[g, nw]
            m = jnp.maximum(jnp.max(s, axis=0), sx)
            p = jnp.exp(s - m[None])
            px = jnp.exp(sx - m)
            parts.append((m, jnp.sum(p, axis=0) + px, jnp.sum(p * vs, axis=0) + px * vx))
        mx = jnp.maximum(jnp.maximum(parts[0][0], parts[1][0]), parts[2][0])
        den = jnp.zeros_like(mx)
        num = jnp.zeros_like(mx)
        for m, l, acc in parts:
            e = jnp.exp(m - mx)
            den = den + e * l
            num = num + e * acc
        o_ref[0, t] = num * (1.0 / den)


def _attn_sample(q, kv_new, cache, brev, *, steps):
    nb, n_old = cache.shape[0], cache.shape[1]
    nw = N_WIN
    assert n_old == nw * DILATIONS[2] and steps == DILATIONS[1] and DILATIONS[0] == 1
    tail = (2, KV_HEADS, ATT_HEAD_DIM)
    c1 = cache.reshape(nb, n_old // DILATIONS[1], DILATIONS[1], *tail)
    c2 = cache.reshape(nb, n_old // DILATIONS[2], DILATIONS[2], *tail)
    z3 = (0, 0, 0)
    return pl.pallas_call(
        functools.partial(_attn_sample_kernel, steps=steps),
        grid=(nb,),
        in_specs=[
            pl.BlockSpec((1, N_DIL_GROUPS * steps, KV_HEADS, ATT_HEAD_DIM), lambda b: (b, 0, 0, 0)),
            pl.BlockSpec((1, steps) + tail, lambda b: (b, 0) + z3),
            pl.BlockSpec((None, nw, steps) + tail,
                         lambda b: (b, n_old // (nw * DILATIONS[1]) - 1, 0) + z3),
            pl.BlockSpec((None, nw, steps) + tail, lambda b: (b, 0, 0) + z3),
            pl.BlockSpec(brev.shape, lambda b: (0, 0, 0, 0)),
        ],
        out_specs=pl.BlockSpec((1, steps, KV_HEADS, ATT_HEAD_DIM), lambda b: (b, 0, 0, 0)),
        out_shape=jax.ShapeDtypeStruct((nb, steps, KV_HEADS, ATT_HEAD_DIM), F32),
        scratch_shapes=[pltpu.VMEM((nw + SUBLANES,) + tail, F32)],
        compiler_params=_params("arbitrary"),
        name="attn_sample",
    )(q, kv_new, c1, c2, brev)


def _out_kernel(o_ref, gate_ref, h_ref, wo_ref, fn_ref, y_ref):
    yb = (o_ref[...] * _silu(gate_ref[...].astype(F32))).astype(BF16)
    h2 = h_ref[...] + _dot(yb, wo_ref[...])
    y_ref[...] = h2 * _rms_scale(h2) * fn_ref[...]


def _out_proj(o, gate, h1, wo, fn, *, tm):
    rows = o.shape[0]
    row_blk = pl.BlockSpec((tm, D_MODEL), lambda i: (i, 0))
    return pl.pallas_call(
        _out_kernel,
        grid=(rows // tm,),
        in_specs=[row_blk, row_blk, row_blk, _const_spec(wo.shape), _const_spec((1, D_MODEL))],
        out_specs=row_blk,
        out_shape=jax.ShapeDtypeStruct((rows, D_MODEL), F32),
        compiler_params=_params("arbitrary"),
        name="out_proj",
    )(o, gate, h1, wo, fn)


def _t5_bucket(dist):
    max_exact = N_BUCKETS // 2
    n = jnp.maximum(dist, 0)
    nf = jnp.maximum(n, 1).astype(F32)
    large = max_exact + (jnp.log(nf / max_exact) / math.log(MAX_DISTANCE / max_exact)
                         * (N_BUCKETS - max_exact)).astype(jnp.int32)
    large = jnp.minimum(large, N_BUCKETS - 1)
    return jnp.where(n < max_exact, n, large)


def _group_bias(rel_bias, g):
    dist = jnp.arange(N_WIN + 1, dtype=jnp.int32) * DILATIONS[g]
    cols = rel_bias[:, g * KV_HEADS:(g + 1) * KV_HEADS].astype(F32)
    return cols[_t5_bucket(dist)]


def _banded_bias(bvec):
    nw = N_WIN
    period = 3 * nw
    g = jnp.concatenate([bvec[::-1].T, jnp.full((KV_HEADS, period - nw - 1), NEG_INF, F32)], axis=1)
    skew = jnp.tile(g, (1, nw))[:, :nw * (period - 1)].reshape(KV_HEADS, nw, period - 1)
    return skew[:, :, :2 * nw]


def _bias_tables(rel_bias):
    nw = N_WIN
    masked = jnp.full((KV_HEADS, nw, nw), NEG_INF, F32)
    prompt, sample = [], []
    for g in range(N_DIL_GROUPS):
        bvec = _group_bias(rel_bias, g)
        band = _banded_bias(bvec)
        first = jnp.concatenate([masked, band[:, :, nw:]], axis=2)
        prompt.append(jnp.stack([band, first]))
        sample.append(jnp.broadcast_to(bvec[::-1][:, :, None], (nw + 1, KV_HEADS, LANES)))
    return jnp.stack(prompt), jnp.stack(sample)


def _head_expand_matrix():
    k = jnp.arange(2 * LANES) % LANES
    j = jnp.arange(SSM_INNER) // SSM_HEAD_DIM
    return (k[:, None] == j[None, :]).astype(BF16)


def _pad_lanes(v):
    return jnp.pad(v.astype(F32), (0, LANES - v.shape[0]))[None]


def kernel(x_prompt, x_sample, state_ssm, state_conv, cache_kv, a_norm, a_w_in, a_conv_w, a_conv_b,
           a_dt_bias, a_A_log, a_D, a_gate_norm, a_w_out, rel_bias, kv_norm, w_kv, b_norm, b_w_in,
           b_w_out, final_norm):
    bp, sp, _ = x_prompt.shape
    db, steps, _ = x_sample.shape
    assert a_w_in.shape[0] == 1 and b_w_in.shape[0] == 1, "one A layer and one B layer"
    assert sp >= MAX_DISTANCE and sp % (N_WIN * DILATIONS[2]) == 0

    n_zx = SSM_INNER + CONV_DIM
    w_a = jnp.concatenate([a_w_in[0, :, :n_zx],
                           jnp.pad(a_w_in[0, :, n_zx:], ((0, 0), (0, LANES - SSM_HEADS)))],
                          axis=1).astype(BF16)
    dtb = _pad_lanes(a_dt_bias[0])
    a_pad = _pad_lanes(-jnp.exp(a_A_log[0].astype(F32)))
    d_exp = jnp.repeat(a_D[0].astype(F32), SSM_HEAD_DIM)[None]
    a_in = functools.partial(_a_in_proj, g=a_norm[0][None], w=w_a, cw=a_conv_w[0],
                             cb=a_conv_b[0][None], dtb=dtb, a_pad=a_pad)
    gg, wo_a = a_gate_norm[0][None], a_w_out[0].astype(BF16)
    b_in = functools.partial(_b_in_proj, kvn=kv_norm[None], wkv=w_kv.astype(BF16),
                             bn=b_norm[0][None], wbi=b_w_in[0].astype(BF16))
    out = functools.partial(_out_proj, wo=b_w_out[0].astype(BF16), fn=final_norm[None])
    bias_prompt, bias_sample = _bias_tables(rel_bias)
    n_keep = min(MAX_DISTANCE, sp)

    tm = ROW_TILE
    assert sp % tm == 0 and n_keep % tm == 0
    conv0 = jnp.zeros((bp, CONV_WIDTH - 1, CONV_DIM), F32)
    z_p, act_p, dt_p, _, conv_p = a_in(x_prompt, conv0, tm=tm, shift=1)
    h1_p, st_p = _ssd_prompt(act_p, dt_p, z_p, x_prompt, a_pad, d_exp, _head_expand_matrix(),
                             gg, wo_a, tm=tm)
    rows_p = bp * sp
    h1_p = h1_p.reshape(rows_p, D_MODEL)
    kv_p, kv_tail, q_p, gate_p = b_in(h1_p, tm=tm, seq_tiles=sp // tm, keep_tiles=n_keep // tm)
    o_p = _attn_prompt(q_p.reshape(bp, sp, Q_WIDTH), kv_p.reshape(bp, sp, 2 * ATT_WIDTH), bias_prompt)
    y_prompt = out(o_p.reshape(rows_p, ATT_WIDTH), gate_p, h1_p, tm=tm).reshape(bp, sp, D_MODEL)

    rows_s = steps * db
    x_s = x_sample.transpose(1, 0, 2).reshape(1, rows_s, D_MODEL)
    cprev_s = state_conv[0].astype(F32).transpose(1, 0, 2).reshape(1, (CONV_WIDTH - 1) * db, CONV_DIM)
    z_s, act_s, dt_s, da_s, conv_s = a_in(x_s, cprev_s, tm=rows_s, shift=db)
    act_s = act_s[0].astype(F32)
    xm = (act_s[:, :SSM_INNER].reshape(steps, db, SSM_HEADS, SSM_HEAD_DIM)
          .transpose(1, 3, 2, 0).reshape(db, SSM_HEAD_DIM, SSM_HEADS * steps))
    per_column = lambda v: jnp.broadcast_to(
        v.reshape(steps, db, SSM_GROUPS, 1, SSM_STATE).transpose(1, 2, 3, 0, 4),
        (db, SSM_GROUPS, SSM_HEADS_PER_GROUP, steps, SSM_STATE)).reshape(db, SSM_HEADS * steps, SSM_STATE)
    gn = SSM_GROUPS * SSM_STATE
    bbig = per_column(act_s[:, SSM_INNER:SSM_INNER + gn])
    cbig = per_column(act_s[:, SSM_INNER + gn:])
    per_step = lambda v: (v[0, :, :SSM_HEADS].reshape(steps, db, SSM_HEADS)
                          .transpose(1, 2, 0).reshape(db, SSM_HEADS * steps))
    a_lane = jnp.repeat(-jnp.exp(a_A_log[0].astype(F32)), steps)[None]
    d_lane = jnp.repeat(a_D[0].astype(F32), steps)[None]
    ym, st_s = _ssd_sample(per_step(da_s), per_step(dt_s)[:, None, :], state_ssm[0].astype(F32), xm,
                           bbig, cbig, a_lane, d_lane, steps=steps)
    y_s = (ym.reshape(db, SSM_HEAD_DIM, SSM_HEADS, steps).transpose(3, 0, 2, 1)
           .reshape(rows_s, SSM_INNER))
    h1_s = _a_out_proj(y_s, z_s[0], x_s[0], gg, wo_a)
    _, kv_s, q_s, gate_s = b_in(h1_s, tm=rows_s, seq_tiles=1, keep_tiles=1)
    kv_new = kv_s.reshape(steps, db, 2, KV_HEADS, ATT_HEAD_DIM).transpose(1, 0, 2, 3, 4)
    q_s = (q_s.reshape(steps, db, N_DIL_GROUPS, KV_HEADS, ATT_HEAD_DIM)
           .transpose(1, 2, 0, 3, 4).reshape(db, N_DIL_GROUPS * steps, KV_HEADS, ATT_HEAD_DIM))
    o_s = _attn_sample(q_s, kv_new, cache_kv.astype(F32), bias_sample, steps=steps)
    y_s2 = out(o_s.transpose(1, 0, 2, 3).reshape(rows_s, ATT_WIDTH), gate_s, h1_s, tm=rows_s)
    y_sample = y_s2.reshape(steps, db, D_MODEL).transpose(1, 0, 2)

    ssm_prompt = st_p.reshape(1, bp, SSM_HEADS, SSM_HEAD_DIM, SSM_STATE).astype(state_ssm.dtype)
    ssm_sample = st_s[None].astype(state_ssm.dtype)
    conv_prompt = conv_p[None].astype(state_conv.dtype)
    conv_sample = (conv_s.reshape(CONV_WIDTH - 1, db, CONV_DIM).transpose(1, 0, 2)[None]
                   .astype(state_conv.dtype))
    kv_prompt = kv_tail.reshape(bp, n_keep, 2, KV_HEADS, ATT_HEAD_DIM).astype(cache_kv.dtype)
    kv_sample = kv_new.astype(cache_kv.dtype)
    return (y_prompt, y_sample, ssm_prompt, ssm_sample, conv_prompt, conv_sample, kv_prompt, kv_sample)
```
